```python
import math
import jax, jax.numpy as jnp
from jax import lax
import numpy as np

D_MODEL = 2048
BATCH = 2
SEQ = 4096
DEPTH = 1
DEC_BATCH = 8
DEC_SEQ = 8
PAST_LEN = 16384
PAGE_SIZE = 128

D_MIX = D_MODEL
W_A = D_MIX // 2
W_B = D_MIX - W_A
HD_A = 128
H_A = W_A // HD_A
CHUNK = 128
H_B = 8
DV_B = W_B // H_B
DH_B = DV_B // 2
D_IN = 2 * W_A + 3 * W_B
D_FF = (11 * D_MODEL) // 4
CONV_W = 3
Q_BLOCK = 128
EPS = 1e-6
SPARE_NUM = 5
SPARE_DEN = 4

kernel_name = 'hymba_gmlp_diffattn_convffn_step'


def rmsnorm(x, g):
    xf = x.astype(jnp.float32)
    y = xf * lax.rsqrt(jnp.mean(xf * xf, axis=-1, keepdims=True) + EPS)
    return (y * g.astype(jnp.float32)).astype(x.dtype)


def diff_lambda(lq1, lk1, lq2, lk2, lam_init):
    f = lambda a: a.astype(jnp.float32)
    return jnp.exp(jnp.sum(f(lq1) * f(lk1))) - jnp.exp(jnp.sum(f(lq2) * f(lk2))) + lam_init


def diff_attend(q, k, v, q_pos, k_pos, lam):
    scale = DH_B ** -0.5
    mask = k_pos[None, :] <= q_pos[:, None]

    def probs(qh, kh):
        s = jnp.einsum('bqhd,bkhd->bhqk', qh, kh).astype(jnp.float32) * scale
        return jax.nn.softmax(jnp.where(mask, s, -jnp.inf), axis=-1)

    w = probs(q[..., :DH_B], k[..., :DH_B]) - lam * probs(q[..., DH_B:], k[..., DH_B:])
    return jnp.einsum('bhqk,bkhe->bqhe', w.astype(v.dtype), v)


def prompt_attend(q, k, v, lam):
    B, S = q.shape[:2]
    nb = S // Q_BLOCK
    qb = q.reshape(B, nb, Q_BLOCK, H_B, DV_B).transpose(1, 0, 2, 3, 4)
    k_pos = jnp.arange(S)

    def block(args):
        qi, start = args
        return diff_attend(qi, k, v, start + jnp.arange(Q_BLOCK), k_pos, lam)

    o = lax.map(block, (qb, jnp.arange(nb) * Q_BLOCK))
    return o.transpose(1, 0, 2, 3, 4).reshape(B, S, H_B, DV_B)


def sample_attend(q, k_new, v_new, k_cache, v_cache, page_table, lam):
    DB, T = q.shape[:2]
    past = page_table.shape[1] * PAGE_SIZE
    k_past = k_cache[page_table].reshape(DB, past, H_B, DV_B)
    v_past = v_cache[page_table].reshape(DB, past, H_B, DV_B)
    k = jnp.concatenate([k_past, k_new.astype(k_past.dtype)], axis=1)
    v = jnp.concatenate([v_past, v_new.astype(v_past.dtype)], axis=1)
    return diff_attend(q, k, v, past + jnp.arange(T), jnp.arange(past + T), lam)


def chunk_gmlp(u, va, w_s, b_s):
    B, T = va.shape[:2]
    pad = (-T) % CHUNK
    nc = (T + pad) // CHUNK
    vc = jnp.pad(va, ((0, 0), (0, pad), (0, 0), (0, 0))).reshape(B, nc, CHUNK, H_A, HD_A)
    w = w_s * jnp.tril(jnp.ones((CHUNK, CHUNK), w_s.dtype))
    mixed = jnp.einsum('hts,bcshd->bcthd', w, vc) + b_s.T[:, :, None]
    mixed = mixed.reshape(B, nc * CHUNK, H_A, HD_A)[:, :T]
    return (u.reshape(B, T, H_A, HD_A) * mixed).reshape(B, T, W_A)


def causal_dwconv(hist, up, w_conv, b_conv):
    T = up.shape[1]
    hp = jnp.concatenate([hist.astype(up.dtype), up], axis=1)
    out = b_conv + sum(hp[:, j:j + T] * w_conv[j] for j in range(CONV_W))
    return out, hp[:, T:]


def layer(x, conv_hist, attend, lp, lam, lam_init):
    B, T, _ = x.shape
    z = rmsnorm(x, lp['g_pre_mix']) @ lp['w_in']
    u, va, q, k, vb = jnp.split(z, [W_A, 2 * W_A, 2 * W_A + W_B, 2 * W_A + 2 * W_B], axis=-1)
    u = jax.nn.gelu(u)
    va = rmsnorm(jax.nn.gelu(va).reshape(B, T, H_A, HD_A), lp['g_v_a'])
    q = q.reshape(B, T, H_B, DV_B)
    k = k.reshape(B, T, H_B, DV_B)
    vb = vb.reshape(B, T, H_B, DV_B)
    a_out = chunk_gmlp(u, va, lp['w_s'], lp['b_s'])
    o = attend(q, k, vb, lam)
    b_out = (rmsnorm(o, lp['g_subln']) * (1.0 - lam_init)).reshape(B, T, W_B)
    mix = jnp.concatenate([a_out, b_out.astype(a_out.dtype)], axis=-1) @ lp['w_out']
    h = x + rmsnorm(mix, lp['g_post_mix'])
    up = rmsnorm(h, lp['g_pre_ffn']) @ lp['w_up']
    c, new_hist = causal_dwconv(conv_hist, up, lp['w_conv'], lp['b_conv'])
    gate, val = jnp.split(c, 2, axis=-1)
    y = (jax.nn.gelu(gate) * val) @ lp['w_down']
    return h + rmsnorm(y, lp['g_post_ffn']), k, vb, new_hist, va


def setup_inputs(seed: int = 0) -> dict:
    key = jax.random.key(seed)
    ks = jax.random.split(key, 26)
    n_pages = PAST_LEN // PAGE_SIZE
    n_phys = (DEC_BATCH * n_pages * SPARE_NUM) // SPARE_DEN
    f32 = jnp.float32
    nrm = lambda k, shape, s: jax.random.normal(k, shape, f32) * s
    gain = lambda k, n: 1.0 + 0.05 * jax.random.normal(k, (DEPTH, n), f32)
    perm = jax.random.permutation(ks[5], n_phys)[:DEC_BATCH * n_pages]
    return {
        'x_prompt': nrm(ks[0], (BATCH, SEQ, D_MODEL), 1.0),
        'x_sample': nrm(ks[1], (DEC_BATCH, DEC_SEQ, D_MODEL), 1.0),
        'cache_k': nrm(ks[2], (DEPTH, n_phys, PAGE_SIZE, H_B, DV_B), 1.0),
        'cache_v': nrm(ks[3], (DEPTH, n_phys, PAGE_SIZE, H_B, DV_B), 1.0),
        'state_conv': nrm(ks[4], (DEPTH, DEC_BATCH, CONV_W - 1, 2 * D_FF), 1.0),
        'page_table': perm.reshape(DEC_BATCH, n_pages).astype(jnp.int32),
        'g_pre_mix': gain(ks[6], D_MODEL),
        'w_in': nrm(ks[7], (DEPTH, D_MODEL, D_IN), D_MODEL ** -0.5),
        'g_v_a': 1.0 + 0.05 * jax.random.normal(ks[8], (DEPTH, H_A, HD_A), f32),
        'w_s': nrm(ks[9], (DEPTH, H_A, CHUNK, CHUNK), CHUNK ** -0.5),
        'b_s': 1.0 + 0.1 * jax.random.normal(ks[10], (DEPTH, H_A, CHUNK), f32),
        'lam_q1': nrm(ks[11], (DEPTH, DH_B), 0.1),
        'lam_k1': nrm(ks[12], (DEPTH, DH_B), 0.1),
        'lam_q2': nrm(ks[13], (DEPTH, DH_B), 0.1),
        'lam_k2': nrm(ks[14], (DEPTH, DH_B), 0.1),
        'g_subln': gain(ks[15], DV_B),
        'w_out': nrm(ks[16], (DEPTH, D_MIX, D_MODEL), D_MIX ** -0.5),
        'g_post_mix': gain(ks[17], D_MODEL),
        'g_pre_ffn': gain(ks[18], D_MODEL),
        'w_up': nrm(ks[19], (DEPTH, D_MODEL, 2 * D_FF), D_MODEL ** -0.5),
        'w_conv': nrm(ks[20], (DEPTH, CONV_W, 2 * D_FF), CONV_W ** -0.5),
        'b_conv': nrm(ks[21], (DEPTH, 2 * D_FF), 0.02),
        'w_down': nrm(ks[22], (DEPTH, D_FF, D_MODEL), D_FF ** -0.5),
        'g_post_ffn': gain(ks[23], D_MODEL),
    }


def reference(x_prompt, x_sample, cache_k, cache_v, state_conv, page_table,
              g_pre_mix, w_in, g_v_a, w_s, b_s, lam_q1, lam_k1, lam_q2, lam_k2,
              g_subln, w_out, g_post_mix, g_pre_ffn, w_up, w_conv, b_conv, w_down,
              g_post_ffn):
    xp, xs = x_prompt, x_sample
    kp_l, vp_l, cp_l, ks_l, vs_l, cs_l, gs_l = [], [], [], [], [], [], []
    for l in range(DEPTH):
        lp = {'g_pre_mix': g_pre_mix[l], 'w_in': w_in[l], 'g_v_a': g_v_a[l],
              'w_s': w_s[l], 'b_s': b_s[l], 'g_subln': g_subln[l], 'w_out': w_out[l],
              'g_post_mix': g_post_mix[l], 'g_pre_ffn': g_pre_ffn[l], 'w_up': w_up[l],
              'w_conv': w_conv[l], 'b_conv': b_conv[l], 'w_down': w_down[l],
              'g_post_ffn': g_post_ffn[l]}
        lam_init = 0.8 - 0.6 * math.exp(-0.3 * l)
        lam = diff_lambda(lam_q1[l], lam_k1[l], lam_q2[l], lam_k2[l], lam_init)
        hist0 = jnp.zeros((xp.shape[0], CONV_W - 1, 2 * D_FF), xp.dtype)
        xp, kp, vp, cp, _ = layer(xp, hist0, prompt_attend, lp, lam, lam_init)
        kc, vc = cache_k[l], cache_v[l]
        attend_s = lambda q, k, v, lm, kc=kc, vc=vc: sample_attend(q, k, v, kc, vc, page_table, lm)
        xs, ksn, vsn, csn, gsn = layer(xs, state_conv[l], attend_s, lp, lam, lam_init)
        kp_l.append(kp); vp_l.append(vp); cp_l.append(cp)
        ks_l.append(ksn); vs_l.append(vsn); cs_l.append(csn); gs_l.append(gsn)
    k_prompt = jnp.stack(kp_l, 0)
    v_prompt = jnp.stack(vp_l, 0)
    conv_prompt = jnp.stack(cp_l, 0)
    k_sample = jnp.stack(ks_l, 0)
    v_sample = jnp.stack(vs_l, 0)
    conv_sample = jnp.stack(cs_l, 0)
    gmlp_v_sample = jnp.stack(gs_l, 0)
    return (xp, xs, k_prompt, v_prompt, conv_prompt, k_sample, v_sample, conv_sample, gmlp_v_sample)
```

```python
import functools
import math

import jax
import jax.numpy as jnp
from jax import lax
from jax.experimental import pallas as pl
from jax.experimental.pallas import tpu as pltpu

F32 = jnp.float32
BF16 = jnp.bfloat16

EPS = 1e-6
HEAD_W = 128
HALF_W = HEAD_W // 2
N_HEADS = 8
GROUP_W = N_HEADS * HEAD_W
CHUNK = 128
PAGE = 128
CONV_W = 3
LAM_INIT = 0.8 - 0.6 * math.exp(0.0)
VMEM_LIMIT = 56 * 1024 * 1024


def _params(sem):
    return pltpu.CompilerParams(dimension_semantics=sem, vmem_limit_bytes=VMEM_LIMIT)


def _rms(x, g):
    return x * lax.rsqrt(jnp.mean(x * x, axis=-1, keepdims=True) + EPS) * g


def _lam(lq1, lk1, lq2, lk2):
    a = jnp.sum(lq1[...] * lk1[...], axis=-1, keepdims=True)
    b = jnp.sum(lq2[...] * lk2[...], axis=-1, keepdims=True)
    return jnp.exp(a) - jnp.exp(b) + LAM_INIT


def _in_proj_kernel(x_ref, g_ref, w_ref, gva_ref, u_ref, va_ref, q_ref, k_ref, v_ref,
                    kb_ref, vb_ref, xn_ref):
    j = pl.program_id(1)

    @pl.when(j == 0)
    def _():
        xn_ref[...] = _rms(x_ref[...], g_ref[...]).astype(BF16)

    z = jnp.dot(xn_ref[...], w_ref[...], preferred_element_type=F32)

    @pl.when(j == 0)
    def _():
        u_ref[...] = jax.nn.gelu(z, approximate=True).astype(u_ref.dtype)

    @pl.when(j == 1)
    def _():
        a = jax.nn.gelu(z, approximate=True)
        for h in range(N_HEADS):
            sl = slice(h * HEAD_W, (h + 1) * HEAD_W)
            va_ref[:, sl] = _rms(a[:, sl], gva_ref[h:h + 1, :]).astype(va_ref.dtype)

    @pl.when(j == 2)
    def _():
        q_ref[...] = (z * (HALF_W ** -0.5)).astype(q_ref.dtype)

    @pl.when(j == 3)
    def _():
        k_ref[...] = z
        kb_ref[...] = z.astype(BF16)

    @pl.when(j == 4)
    def _():
        v_ref[...] = z
        vb_ref[...] = z.astype(BF16)


def _in_proj(x2d, g, w, gva, tm, act_dtype):
    m, d = x2d.shape
    row = lambda i, j: (i, 0)
    blk = pl.BlockSpec((tm, GROUP_W), row)
    return pl.pallas_call(
        _in_proj_kernel,
        grid=(m // tm, 5),
        in_specs=[pl.BlockSpec((tm, d), row),
                  pl.BlockSpec((1, d), lambda i, j: (0, 0)),
                  pl.BlockSpec((d, GROUP_W), lambda i, j: (0, j)),
                  pl.BlockSpec((N_HEADS, HEAD_W), lambda i, j: (0, 0))],
        out_specs=[blk] * 7,
        out_shape=[jax.ShapeDtypeStruct((m, GROUP_W), act_dtype),
                   jax.ShapeDtypeStruct((m, GROUP_W), act_dtype),
                   jax.ShapeDtypeStruct((m, GROUP_W), act_dtype),
                   jax.ShapeDtypeStruct((m, GROUP_W), F32),
                   jax.ShapeDtypeStruct((m, GROUP_W), F32),
                   jax.ShapeDtypeStruct((m, GROUP_W), BF16),
                   jax.ShapeDtypeStruct((m, GROUP_W), BF16)],
        scratch_shapes=[pltpu.VMEM((tm, d), BF16)],
        compiler_params=_params(("arbitrary", "arbitrary")),
        name="in_proj",
    )(x2d, g, w, gva)


def _tril_weights(ws_ref, h):
    r = lax.broadcasted_iota(jnp.int32, (CHUNK, CHUNK), 0)
    c = lax.broadcasted_iota(jnp.int32, (CHUNK, CHUNK), 1)
    return jnp.where(c <= r, ws_ref[h], 0.0).astype(BF16)


def _gmlp_kernel(u_ref, va_ref, ws_ref, bt_ref, o_ref, *, rows, n_chunks):
    for h in range(N_HEADS):
        sl = slice(h * HEAD_W, (h + 1) * HEAD_W)
        w = _tril_weights(ws_ref, h)
        bias = bt_ref[:, h:h + 1]
        for c in range(n_chunks):
            rs = slice(c * rows, (c + 1) * rows)
            vc = va_ref[rs, sl].astype(F32)
            if rows < CHUNK:
                vc = jnp.concatenate([vc, jnp.zeros((CHUNK - rows, HEAD_W), F32)], axis=0)
            mixed = jnp.dot(w, vc.astype(BF16), preferred_element_type=F32) + bias
            o_ref[rs, sl] = (u_ref[rs, sl].astype(F32) * mixed[:rows]).astype(o_ref.dtype)


def _gmlp(u, va, w_s, b_t, rows, tm, out_dtype):
    m = u.shape[0]
    row = lambda i: (i, 0)
    return pl.pallas_call(
        functools.partial(_gmlp_kernel, rows=rows, n_chunks=tm // rows),
        grid=(m // tm,),
        in_specs=[pl.BlockSpec((tm, GROUP_W), row),
                  pl.BlockSpec((tm, GROUP_W), row),
                  pl.BlockSpec((N_HEADS, CHUNK, CHUNK), lambda i: (0, 0, 0)),
                  pl.BlockSpec((CHUNK, N_HEADS), lambda i: (0, 0))],
        out_specs=pl.BlockSpec((tm, GROUP_W), row),
        out_shape=jax.ShapeDtypeStruct((m, GROUP_W), out_dtype),
        compiler_params=_params(("arbitrary",)),
        name="gmlp",
    )(u, va, w_s, b_t)


def _split_halves(q):
    lane = lax.broadcasted_iota(jnp.int32, q.shape, 1)
    zero = jnp.zeros_like(q)
    return jnp.concatenate([jnp.where(lane < HALF_W, q, zero),
                            jnp.where(lane >= HALF_W, q, zero)], axis=0)


def _subln(o, lam, g):
    t = o.shape[0] // 2
    d = o[:t] - lam * o[t:]
    return _rms(d, g) * (1.0 - LAM_INIT)


def _prompt_attn_kernel(lq1, lk1, lq2, lk2, g_ref, q_ref, k_ref, v_ref, o_ref,
                        m_ref, l_ref, acc_ref, *, tq):
    qi = pl.program_id(2)
    qq = _split_halves(q_ref[0])
    m_ref[...] = jnp.full(m_ref.shape, -jnp.inf, F32)
    l_ref[...] = jnp.zeros(l_ref.shape, F32)
    acc_ref[...] = jnp.zeros(acc_ref.shape, F32)

    def step(kb, masked):
        start = pl.multiple_of(kb * tq, tq)
        k = k_ref[0, pl.ds(start, tq), :]
        v = v_ref[0, pl.ds(start, tq), :]
        s = lax.dot_general(qq, k, (((1,), (1,)), ((), ())), preferred_element_type=F32)
        if masked:
            r = lax.broadcasted_iota(jnp.int32, s.shape, 0)
            c = lax.broadcasted_iota(jnp.int32, s.shape, 1)
            r = jnp.where(r >= tq, r - tq, r)
            s = jnp.where(c <= r, s, -jnp.inf)
        m_old = m_ref[...]
        m_new = jnp.maximum(m_old, jnp.max(s, axis=-1, keepdims=True))
        alpha = jnp.exp(m_old - m_new)
        p = jnp.exp(s - m_new)
        l_ref[...] = alpha * l_ref[...] + jnp.sum(p, axis=-1, keepdims=True)
        acc_ref[...] = alpha * acc_ref[...] + jnp.dot(p.astype(BF16), v,
                                                      preferred_element_type=F32)
        m_ref[...] = m_new

    def body(kb, carry):
        step(kb, False)
        return carry

    lax.fori_loop(0, qi, body, 0)
    step(qi, True)

    lam = _lam(lq1, lk1, lq2, lk2)
    o = acc_ref[...] / l_ref[...]
    o_ref[0] = _subln(o, lam, g_ref[...]).astype(o_ref.dtype)


def _prompt_attn(q, k, v, lams, g_subln, tq):
    b, s, _ = q.shape
    lam_spec = pl.BlockSpec((1, HALF_W), lambda bi, h, qi: (0, 0))
    qspec = pl.BlockSpec((1, tq, HEAD_W), lambda bi, h, qi: (bi, qi, h))
    kvspec = pl.BlockSpec((1, s, HEAD_W), lambda bi, h, qi: (bi, 0, h))
    return pl.pallas_call(
        functools.partial(_prompt_attn_kernel, tq=tq),
        grid=(b, N_HEADS, s // tq),
        in_specs=[lam_spec] * 4 + [pl.BlockSpec((1, HEAD_W), lambda bi, h, qi: (0, 0)),
                                   qspec, kvspec, kvspec],
        out_specs=qspec,
        out_shape=jax.ShapeDtypeStruct((b, s, GROUP_W), BF16),
        scratch_shapes=[pltpu.VMEM((2 * tq, 1), F32), pltpu.VMEM((2 * tq, 1), F32),
                        pltpu.VMEM((2 * tq, HEAD_W), F32)],
        compiler_params=_params(("arbitrary", "arbitrary", "arbitrary")),
        name="prompt_attn",
    )(*lams, g_subln, q, k, v)


SAMPLE_ROWS = 2 * N_HEADS * 8


def _sample_attn_kernel(pt_ref, lq1, lk1, lq2, lk2, g_ref, q_ref, kn_ref, vn_ref, *rest,
                        pages, t_new):
    k_refs = rest[:pages]
    v_refs = rest[pages:2 * pages]
    o_ref = rest[2 * pages]
    qbd_ref, m_ref, l_ref, acc_ref = rest[2 * pages + 1:]
    c = pl.program_id(1)
    nc = pl.num_programs(1)

    @pl.when(c == 0)
    def _():
        q = q_ref[...].astype(F32)
        qt = jnp.concatenate([q] * (SAMPLE_ROWS // t_new), axis=0)
        r = lax.broadcasted_iota(jnp.int32, qt.shape, 0)
        ln = lax.broadcasted_iota(jnp.int32, qt.shape, 1)
        qbd_ref[...] = jnp.where((r // t_new) == (ln // HALF_W), qt, 0.0).astype(BF16)
        m_ref[...] = jnp.full(m_ref.shape, -jnp.inf, F32)
        l_ref[...] = jnp.zeros(l_ref.shape, F32)
        acc_ref[...] = jnp.zeros(acc_ref.shape, F32)

    qbd = qbd_ref[...]

    def online(s_blocks, v_blocks):
        s = jnp.concatenate(s_blocks, axis=1) if len(s_blocks) > 1 else s_blocks[0]
        m_old = m_ref[...]
        m_new = jnp.maximum(m_old, jnp.max(s, axis=-1, keepdims=True))
        alpha = jnp.exp(m_old - m_new)
        p = jnp.exp(s - m_new)
        l_ref[...] = alpha * l_ref[...] + jnp.sum(p, axis=-1, keepdims=True)
        pv = None
        off = 0
        for vb in v_blocks:
            n = vb.shape[0]
            d = jnp.dot(p[:, off:off + n].astype(BF16), vb, preferred_element_type=F32)
            pv = d if pv is None else pv + d
            off += n
        acc_ref[...] = alpha * acc_ref[...] + pv
        m_ref[...] = m_new

    def scores(kb):
        return lax.dot_general(qbd, kb, (((1,), (1,)), ((), ())), preferred_element_type=F32)

    kbs = [r[0].astype(BF16) for r in k_refs]
    vbs = [r[0].astype(BF16) for r in v_refs]
    online([scores(kb) for kb in kbs], vbs)

    @pl.when(c == nc - 1)
    def _():
        pad = jnp.zeros((PAGE - t_new, GROUP_W), F32)
        kn = jnp.concatenate([kn_ref[...], pad], axis=0).astype(BF16)
        vn = jnp.concatenate([vn_ref[...], pad], axis=0).astype(BF16)
        s = scores(kn)
        r = lax.broadcasted_iota(jnp.int32, s.shape, 0)
        col = lax.broadcasted_iota(jnp.int32, s.shape, 1)
        s = jnp.where(col <= (r % t_new), s, -jnp.inf)
        online([s], [vn])

        lam = _lam(lq1, lk1, lq2, lk2)
        o = acc_ref[...] / l_ref[...]
        for h in range(N_HEADS):
            blk = o[2 * t_new * h:2 * t_new * (h + 1), h * HEAD_W:(h + 1) * HEAD_W]
            o_ref[:, h * HEAD_W:(h + 1) * HEAD_W] = _subln(blk, lam, g_ref[...])


def _sample_attn(q, k_new, v_new, cache_k, cache_v, page_table, lams, g_subln, t_new, pages):
    n_b, n_pages = page_table.shape
    ck = cache_k.reshape(cache_k.shape[0], PAGE, GROUP_W)
    cv = cache_v.reshape(cache_v.shape[0], PAGE, GROUP_W)
    pt = page_table.reshape(-1)
    const = lambda shape: pl.BlockSpec(shape, lambda b, c, pt_ref: (0, 0))
    rows = pl.BlockSpec((t_new, GROUP_W), lambda b, c, pt_ref: (b, 0))

    def page_spec(j):
        return pl.BlockSpec(
            (1, PAGE, GROUP_W),
            lambda b, c, pt_ref: (pt_ref[b * n_pages + c * pages + j], 0, 0))

    grid_spec = pltpu.PrefetchScalarGridSpec(
        num_scalar_prefetch=1,
        grid=(n_b, n_pages // pages),
        in_specs=[const((1, HALF_W))] * 4 + [const((1, HEAD_W)), rows, rows, rows]
                 + [page_spec(j) for j in range(pages)] * 2,
        out_specs=rows,
        scratch_shapes=[pltpu.VMEM((SAMPLE_ROWS, GROUP_W), BF16),
                        pltpu.VMEM((SAMPLE_ROWS, 1), F32),
                        pltpu.VMEM((SAMPLE_ROWS, 1), F32),
                        pltpu.VMEM((SAMPLE_ROWS, GROUP_W), F32)])
    return pl.pallas_call(
        functools.partial(_sample_attn_kernel, pages=pages, t_new=t_new),
        grid_spec=grid_spec,
        out_shape=jax.ShapeDtypeStruct((n_b * t_new, GROUP_W), F32),
        compiler_params=_params(("arbitrary", "arbitrary")),
        name="sample_attn",
    )(pt, *lams, g_subln, q, k_new, v_new, *([ck] * pages), *([cv] * pages))


def _out_proj_kernel(a_ref, b_ref, w_ref, x_ref, gpost_ref, gpre_ref, h_ref, hn_ref):
    mix = jnp.dot(a_ref[...].astype(BF16), w_ref[:GROUP_W, :], preferred_element_type=F32)
    mix += jnp.dot(b_ref[...].astype(BF16), w_ref[GROUP_W:, :], preferred_element_type=F32)
    h = x_ref[...] + _rms(mix, gpost_ref[...])
    h_ref[...] = h
    hn_ref[...] = _rms(h, gpre_ref[...]).astype(hn_ref.dtype)


def _out_proj(a, b, w, x2d, g_post, g_pre, tm):
    m, d = x2d.shape
    row = lambda i: (i, 0)
    const = lambda i: (0, 0)
    return pl.pallas_call(
        _out_proj_kernel,
        grid=(m // tm,),
        in_specs=[pl.BlockSpec((tm, GROUP_W), row), pl.BlockSpec((tm, GROUP_W), row),
                  pl.BlockSpec(w.shape, const), pl.BlockSpec((tm, d), row),
                  pl.BlockSpec((1, d), const), pl.BlockSpec((1, d), const)],
        out_specs=[pl.BlockSpec((tm, d), row), pl.BlockSpec((tm, d), row)],
        out_shape=[jax.ShapeDtypeStruct((m, d), F32), jax.ShapeDtypeStruct((m, d), BF16)],
        compiler_params=_params(("arbitrary",)),
        name="out_proj",
    )(a, b, w, x2d, g_post, g_pre)


HALO = 16


def _gated(cg, cv):
    return (jax.nn.gelu(cg, approximate=True) * cv).astype(BF16)


def _ffn_finish(acc, h_ref, g_ref, o_ref):
    o_ref[...] = h_ref[...] + _rms(acc, g_ref[...])


def _prompt_ffn_kernel(hn_ref, halo_ref, wg_ref, wv_ref, cwg_ref, cwv_ref, cbg_ref, cbv_ref,
                       wd_ref, h_ref, g_ref, o_ref, lastg_ref, lastv_ref,
                       xh_ref, ug_ref, uv_ref, acc_ref, *, tm, tiles_per_seq):
    i = pl.program_id(0)
    f = pl.program_id(1)

    @pl.when(f == 0)
    def _():
        starts = (i % tiles_per_seq) == 0
        xh_ref[:HALO, :] = jnp.where(starts, jnp.zeros_like(halo_ref[...]), halo_ref[...])
        xh_ref[HALO:, :] = hn_ref[...]
        acc_ref[...] = jnp.zeros(acc_ref.shape, F32)

    xh = xh_ref[...]
    ug_ref[...] = jnp.dot(xh, wg_ref[...], preferred_element_type=F32)
    uv_ref[...] = jnp.dot(xh, wv_ref[...], preferred_element_type=F32)

    def conv(u_ref, cw_ref, cb_ref):
        out = cb_ref[...]
        for j in range(CONV_W):
            lo = HALO - (CONV_W - 1) + j
            out = out + u_ref[lo:lo + tm, :] * cw_ref[j:j + 1, :]
        return out

    act = _gated(conv(ug_ref, cwg_ref, cbg_ref), conv(uv_ref, cwv_ref, cbv_ref))
    acc_ref[...] += jnp.dot(act, wd_ref[...], preferred_element_type=F32)
    lastg_ref[0] = ug_ref[HALO + tm - (CONV_W - 1):, :]
    lastv_ref[0] = uv_ref[HALO + tm - (CONV_W - 1):, :]

    @pl.when(f == pl.num_programs(1) - 1)
    def _():
        _ffn_finish(acc_ref[...], h_ref, g_ref, o_ref)


def _prompt_ffn(hn, h, w_up, w_conv, b_conv, w_down, g_post, tm, tf, seq):
    m, d = h.shape
    d_ff = w_down.shape[0]
    nf = d_ff // tf
    nt = m // tm
    row = lambda i, f: (i, 0)
    const = lambda i, f: (0, 0)
    gate = lambda i, f: (0, f)
    val = lambda i, f: (0, nf + f)
    last = pl.BlockSpec((1, CONV_W - 1, tf), lambda i, f: (i, 0, f))
    return pl.pallas_call(
        functools.partial(_prompt_ffn_kernel, tm=tm, tiles_per_seq=seq // tm),
        grid=(nt, nf),
        in_specs=[pl.BlockSpec((tm, d), row),
                  pl.BlockSpec((HALO, d), lambda i, f: (jnp.maximum(i * (tm // HALO) - 1, 0), 0)),
                  pl.BlockSpec((d, tf), gate), pl.BlockSpec((d, tf), val),
                  pl.BlockSpec((CONV_W, tf), gate), pl.BlockSpec((CONV_W, tf), val),
                  pl.BlockSpec((1, tf), gate), pl.BlockSpec((1, tf), val),
                  pl.BlockSpec((tf, d), lambda i, f: (f, 0)),
                  pl.BlockSpec((tm, d), row),
                  pl.BlockSpec((1, d), const)],
        out_specs=[pl.BlockSpec((tm, d), row), last, last],
        out_shape=[jax.ShapeDtypeStruct((m, d), F32),
                   jax.ShapeDtypeStruct((nt, CONV_W - 1, d_ff), F32),
                   jax.ShapeDtypeStruct((nt, CONV_W - 1, d_ff), F32)],
        scratch_shapes=[pltpu.VMEM((tm + HALO, d), BF16),
                        pltpu.VMEM((tm + HALO, tf), F32),
                        pltpu.VMEM((tm + HALO, tf), F32),
                        pltpu.VMEM((tm, d), F32)],
        compiler_params=_params(("arbitrary", "arbitrary")),
        name="prompt_ffn",
    )(hn, hn, w_up, w_up, w_conv, w_conv, b_conv, b_conv, w_down, h, g_post)


def _sample_ffn_kernel(hn_ref, histg_ref, histv_ref, wg_ref, wv_ref, cwg_ref, cwv_ref,
                       cbg_ref, cbv_ref, wd_ref, h_ref, g_ref, o_ref, newg_ref, newv_ref,
                       acc_ref, hpg_ref, hpv_ref, *, n_b, t_new):
    f = pl.program_id(0)
    nh = CONV_W - 1

    @pl.when(f == 0)
    def _():
        acc_ref[...] = jnp.zeros(acc_ref.shape, F32)

    hn = hn_ref[...]
    ug = jnp.dot(hn, wg_ref[...], preferred_element_type=F32)
    uv = jnp.dot(hn, wv_ref[...], preferred_element_type=F32)

    def conv(u, hist_ref, cw_ref, cb_ref, new_ref, hp_ref):
        outs = []
        for b in range(n_b):
            hp_ref[b, t_new - nh:t_new, :] = hist_ref[b]
            hp_ref[b, t_new:, :] = u[b * t_new:(b + 1) * t_new, :]
            c = cb_ref[...]
            for j in range(CONV_W):
                lo = t_new - nh + j
                c = c + hp_ref[b, lo:lo + t_new, :] * cw_ref[j:j + 1, :]
            outs.append(c)
            new_ref[b] = hp_ref[b, 2 * t_new - nh:, :]
        return jnp.concatenate(outs, axis=0)

    act = _gated(conv(ug, histg_ref, cwg_ref, cbg_ref, newg_ref, hpg_ref),
                 conv(uv, histv_ref, cwv_ref, cbv_ref, newv_ref, hpv_ref))
    acc_ref[...] += jnp.dot(act, wd_ref[...], preferred_element_type=F32)

    @pl.when(f == pl.num_programs(0) - 1)
    def _():
        _ffn_finish(acc_ref[...], h_ref, g_ref, o_ref)


def _sample_ffn(hn, h, hist, w_up, w_conv, b_conv, w_down, g_post, tf, n_b, t_new):
    m, d = h.shape
    d_ff = w_down.shape[0]
    nf = d_ff // tf
    const = lambda f: (0, 0)
    gate = lambda f: (0, f)
    val = lambda f: (0, nf + f)
    hist_g = pl.BlockSpec((n_b, CONV_W - 1, tf), lambda f: (0, 0, f))
    hist_v = pl.BlockSpec((n_b, CONV_W - 1, tf), lambda f: (0, 0, nf + f))
    return pl.pallas_call(
        functools.partial(_sample_ffn_kernel, n_b=n_b, t_new=t_new),
        grid=(nf,),
        in_specs=[pl.BlockSpec((m, d), const), hist_g, hist_v,
                  pl.BlockSpec((d, tf), gate), pl.BlockSpec((d, tf), val),
                  pl.BlockSpec((CONV_W, tf), gate), pl.BlockSpec((CONV_W, tf), val),
                  pl.BlockSpec((1, tf), gate), pl.BlockSpec((1, tf), val),
                  pl.BlockSpec((tf, d), lambda f: (f, 0)),
                  pl.BlockSpec((m, d), const), pl.BlockSpec((1, d), const)],
        out_specs=[pl.BlockSpec((m, d), const), hist_g, hist_g],
        out_shape=[jax.ShapeDtypeStruct((m, d), F32),
                   jax.ShapeDtypeStruct((n_b, CONV_W - 1, d_ff), F32),
                   jax.ShapeDtypeStruct((n_b, CONV_W - 1, d_ff), F32)],
        scratch_shapes=[pltpu.VMEM((m, d), F32),
                        pltpu.VMEM((n_b, 2 * t_new, tf), F32),
                        pltpu.VMEM((n_b, 2 * t_new, tf), F32)],
        compiler_params=_params(("arbitrary",)),
        name="sample_ffn",
    )(hn, hist, hist, w_up, w_up, w_conv, w_conv, b_conv, b_conv, w_down, h, g_post)


def kernel(x_prompt, x_sample, cache_k, cache_v, state_conv, page_table, g_pre_mix, w_in, g_v_a,
           w_s, b_s, lam_q1, lam_k1, lam_q2, lam_k2, g_subln, w_out, g_post_mix, g_pre_ffn, w_up,
           w_conv, b_conv, w_down, g_post_ffn):
    n_bp, seq, d = x_prompt.shape
    n_bs, t_new, _ = x_sample.shape
    depth = w_in.shape[0]
    assert depth == 1
    d_ff = w_down.shape[1]

    w_in_b = w_in[0].astype(BF16)
    w_out_b = w_out[0].astype(BF16)
    w_up_b = w_up[0].astype(BF16)
    w_down_b = w_down[0].astype(BF16)
    b_t = b_s[0].T
    lams = (lam_q1, lam_k1, lam_q2, lam_k2)
    b_conv2 = b_conv

    xp = x_prompt.reshape(n_bp * seq, d)
    u, va, q, k, v, kb, vb = _in_proj(xp, g_pre_mix, w_in_b, g_v_a[0], 512, BF16)
    a_out = _gmlp(u, va, w_s[0], b_t, CHUNK, 512, BF16)
    sh3 = (n_bp, seq, GROUP_W)
    b_out = _prompt_attn(q.reshape(sh3), kb.reshape(sh3), vb.reshape(sh3), lams, g_subln, 512)
    h, hn = _out_proj(a_out, b_out.reshape(n_bp * seq, GROUP_W), w_out_b, xp,
                      g_post_mix, g_pre_ffn, 512)
    tm_ffn = 512
    yp, lastg, lastv = _prompt_ffn(hn, h, w_up_b, w_conv[0], b_conv2, w_down_b, g_post_ffn,
                                   tm_ffn, 512, seq)
    tps = seq // tm_ffn
    conv_p = jnp.concatenate([lastg[tps - 1::tps], lastv[tps - 1::tps]], axis=-1)

    xs = x_sample.reshape(n_bs * t_new, d)
    us, vas, qs, ks, vs, _, _ = _in_proj(xs, g_pre_mix, w_in_b, g_v_a[0], n_bs * t_new, F32)
    a_s = _gmlp(us, vas, w_s[0], b_t, t_new, n_bs * t_new, F32)
    b_sm = _sample_attn(qs, ks, vs, cache_k[0], cache_v[0], page_table, lams, g_subln, t_new, 8)
    hs, hns = _out_proj(a_s, b_sm, w_out_b, xs, g_post_mix, g_pre_ffn, n_bs * t_new)
    ys, newg, newv = _sample_ffn(hns, hs, state_conv[0], w_up_b, w_conv[0], b_conv2, w_down_b,
                                 g_post_ffn, 512, n_bs, t_new)
    conv_s = jnp.concatenate([newg, newv], axis=-1)

    hsh = (depth, n_bp, seq, N_HEADS, HEAD_W)
    ssh = (depth, n_bs, t_new, N_HEADS, HEAD_W)
    return (yp.reshape(n_bp, seq, d), ys.reshape(n_bs, t_new, d),
            k.reshape(hsh), v.reshape(hsh), conv_p[None],
            ks.reshape(ssh), vs.reshape(ssh), conv_s[None], vas.reshape(ssh))
```

```python
import functools
import math

import jax
import jax.numpy as jnp
from jax import lax
from jax.experimental import pallas as pl
from jax.experimental.pallas import tpu as pltpu

F32 = jnp.float32
BF16 = jnp.bfloat16

EPS = 1e-6
HEAD_W = 128
HALF_W = HEAD_W // 2
N_HEADS = 8
GROUP_W = N_HEADS * HEAD_W
CHUNK = 128
PAGE = 128
CONV_W = 3
LAM_INIT = 0.8 - 0.6 * math.exp(0.0)
Q_SCALE = HALF_W ** -0.5 * math.log2(math.e)
VMEM_LIMIT = 56 * 1024 * 1024


def _params(sem):
    return pltpu.CompilerParams(dimension_semantics=sem, vmem_limit_bytes=VMEM_LIMIT)


def _rms(x, g):
    return x * lax.rsqrt(jnp.mean(x * x, axis=-1, keepdims=True) + EPS) * g


def _lam(lq1, lk1, lq2, lk2):
    a = jnp.sum(lq1[...] * lk1[...], axis=-1, keepdims=True)
    b = jnp.sum(lq2[...] * lk2[...], axis=-1, keepdims=True)
    return jnp.exp(a) - jnp.exp(b) + LAM_INIT


def _in_proj_kernel(x_ref, g_ref, w_ref, gva_ref, u_ref, va_ref, q_ref, k_ref, v_ref,
                    kb_ref, vb_ref, xn_ref):
    j = pl.program_id(1)

    @pl.when(j == 0)
    def _():
        xn_ref[...] = _rms(x_ref[...], g_ref[...]).astype(BF16)

    z = jnp.dot(xn_ref[...], w_ref[...], preferred_element_type=F32)

    @pl.when(j == 0)
    def _():
        u_ref[...] = jax.nn.gelu(z, approximate=True).astype(u_ref.dtype)

    @pl.when(j == 1)
    def _():
        a = jax.nn.gelu(z, approximate=True)
        for h in range(N_HEADS):
            sl = slice(h * HEAD_W, (h + 1) * HEAD_W)
            va_ref[:, sl] = _rms(a[:, sl], gva_ref[h:h + 1, :]).astype(va_ref.dtype)

    @pl.when(j == 2)
    def _():
        q_ref[...] = (z * Q_SCALE).astype(q_ref.dtype)

    @pl.when(j == 3)
    def _():
        k_ref[...] = z
        kb_ref[...] = z.astype(BF16)

    @pl.when(j == 4)
    def _():
        v_ref[...] = z
        vb_ref[...] = z.astype(BF16)


def _in_proj(x2d, g, w, gva, tm, act_dtype):
    m, d = x2d.shape
    row = lambda i, j: (i, 0)
    blk = pl.BlockSpec((tm, GROUP_W), row)
    return pl.pallas_call(
        _in_proj_kernel,
        grid=(m // tm, 5),
        in_specs=[pl.BlockSpec((tm, d), row),
                  pl.BlockSpec((1, d), lambda i, j: (0, 0)),
                  pl.BlockSpec((d, GROUP_W), lambda i, j: (0, j)),
                  pl.BlockSpec((N_HEADS, HEAD_W), lambda i, j: (0, 0))],
        out_specs=[blk] * 7,
        out_shape=[jax.ShapeDtypeStruct((m, GROUP_W), act_dtype),
                   jax.ShapeDtypeStruct((m, GROUP_W), act_dtype),
                   jax.ShapeDtypeStruct((m, GROUP_W), act_dtype),
                   jax.ShapeDtypeStruct((m, GROUP_W), F32),
                   jax.ShapeDtypeStruct((m, GROUP_W), F32),
                   jax.ShapeDtypeStruct((m, GROUP_W), BF16),
                   jax.ShapeDtypeStruct((m, GROUP_W), BF16)],
        scratch_shapes=[pltpu.VMEM((tm, d), BF16)],
        compiler_params=_params(("arbitrary", "arbitrary")),
        name="in_proj",
    )(x2d, g, w, gva)


def _tril_weights(ws_ref, h):
    r = lax.broadcasted_iota(jnp.int32, (CHUNK, CHUNK), 0)
    c = lax.broadcasted_iota(jnp.int32, (CHUNK, CHUNK), 1)
    return jnp.where(c <= r, ws_ref[h], 0.0).astype(BF16)


def _gmlp_kernel(u_ref, va_ref, ws_ref, bt_ref, o_ref, *, rows, n_chunks):
    for h in range(N_HEADS):
        sl = slice(h * HEAD_W, (h + 1) * HEAD_W)
        w = _tril_weights(ws_ref, h)
        bias = bt_ref[:, h:h + 1]
        for c in range(n_chunks):
            rs = slice(c * rows, (c + 1) * rows)
            vc = va_ref[rs, sl].astype(F32)
            if rows < CHUNK:
                vc = jnp.concatenate([vc, jnp.zeros((CHUNK - rows, HEAD_W), F32)], axis=0)
            mixed = jnp.dot(w, vc.astype(BF16), preferred_element_type=F32) + bias
            o_ref[rs, sl] = (u_ref[rs, sl].astype(F32) * mixed[:rows]).astype(o_ref.dtype)


def _gmlp(u, va, w_s, b_t, rows, tm, out_dtype):
    m = u.shape[0]
    row = lambda i: (i, 0)
    return pl.pallas_call(
        functools.partial(_gmlp_kernel, rows=rows, n_chunks=tm // rows),
        grid=(m // tm,),
        in_specs=[pl.BlockSpec((tm, GROUP_W), row),
                  pl.BlockSpec((tm, GROUP_W), row),
                  pl.BlockSpec((N_HEADS, CHUNK, CHUNK), lambda i: (0, 0, 0)),
                  pl.BlockSpec((CHUNK, N_HEADS), lambda i: (0, 0))],
        out_specs=pl.BlockSpec((tm, GROUP_W), row),
        out_shape=jax.ShapeDtypeStruct((m, GROUP_W), out_dtype),
        compiler_params=_params(("arbitrary",)),
        name="gmlp",
    )(u, va, w_s, b_t)


ONES_PAD = 16


def _subln(o, lam, g):
    t = o.shape[0] // 2
    d = o[:t] - lam * o[t:]
    return _rms(d, g) * (1.0 - LAM_INIT)


def _prompt_attn_kernel(lq1, lk1, lq2, lk2, g_ref, q_ref, k_ref, v_ref, o_ref,
                        qt_ref, vt_ref, m_ref, acc_ref, s_ref, *, tq, n_chain):
    qi = pl.program_id(2)

    @pl.when(qi == 0)
    def _():
        ones_row = lax.broadcasted_iota(jnp.int32, (ONES_PAD, tq), 0) == 0
        for c in range(vt_ref.shape[0]):
            vt_ref[c, :HEAD_W, :] = v_ref[0, c * tq:(c + 1) * tq, :].astype(F32).T.astype(BF16)
            vt_ref[c, HEAD_W:, :] = jnp.where(ones_row, 1.0, 0.0).astype(BF16)

    qf = q_ref[0].astype(F32).T
    dim = lax.broadcasted_iota(jnp.int32, qf.shape, 0)
    qt_ref[:, :tq] = jnp.where(dim < HALF_W, qf, 0.0).astype(BF16)
    qt_ref[:, tq:] = jnp.where(dim >= HALF_W, qf, 0.0).astype(BF16)
    m_ref[...] = jnp.full(m_ref.shape, -jnp.inf, F32)
    acc_ref[...] = jnp.zeros(acc_ref.shape, F32)
    cw = 2 * tq // n_chain

    def scores(kb, slot, masked):
        k = k_ref[0, pl.ds(pl.multiple_of(kb * tq, tq), tq), :]
        for c in range(n_chain):
            cs = slice(c * cw, (c + 1) * cw)
            s = jnp.dot(k, qt_ref[:, cs], preferred_element_type=F32)
            if masked:
                key = lax.broadcasted_iota(jnp.int32, s.shape, 0)
                qry = lax.broadcasted_iota(jnp.int32, s.shape, 1) + (c * cw) % tq
                s = jnp.where(key <= qry, s, -jnp.inf)
            s_ref[slot, c] = s

    def absorb(kb, slot):
        vt = vt_ref[kb]
        for c in range(n_chain):
            cs = slice(c * cw, (c + 1) * cw)
            s = s_ref[slot, c]
            m_old = m_ref[:, cs]
            m_new = jnp.maximum(m_old, jnp.max(s, axis=0, keepdims=True))
            alpha = jnp.exp2(m_old - m_new)
            p = jnp.exp2(s - m_new)
            acc_ref[:, cs] = alpha * acc_ref[:, cs] + jnp.dot(vt, p.astype(BF16),
                                                              preferred_element_type=F32)
            m_ref[:, cs] = m_new

    @pl.when(qi == 0)
    def _():
        scores(0, 0, True)

    @pl.when(qi > 0)
    def _():
        scores(0, 0, False)

    n_pair = lax.shift_right_logical(jnp.maximum(qi - 1, 0), 1)

    def pair(t, carry):
        kb = 2 * t
        scores(kb + 1, 1, False)
        absorb(kb, 0)
        scores(kb + 2, 0, False)
        absorb(kb + 1, 1)
        return carry

    lax.fori_loop(0, n_pair, pair, 0)
    left = qi - 2 * n_pair

    @pl.when(left == 0)
    def _():
        absorb(qi, 0)

    @pl.when(left == 1)
    def _():
        scores(qi, 1, True)
        absorb(qi - 1, 0)
        absorb(qi, 1)

    @pl.when(left == 2)
    def _():
        scores(qi - 1, 1, False)
        absorb(qi - 2, 0)
        scores(qi, 0, True)
        absorb(qi - 1, 1)
        absorb(qi, 0)

    lam = _lam(lq1, lk1, lq2, lk2)
    o = acc_ref[:HEAD_W, :] / acc_ref[HEAD_W:HEAD_W + 1, :]
    d = o[:, :tq] - lam * o[:, tq:]
    dn = d * lax.rsqrt(jnp.mean(d * d, axis=0, keepdims=True) + EPS)
    o_ref[0] = (dn.T * g_ref[...] * (1.0 - LAM_INIT)).astype(o_ref.dtype)


def _prompt_attn(q, k, v, lams, g_subln, tq, n_chain):
    b, s, _ = q.shape
    lam_spec = pl.BlockSpec((1, HALF_W), lambda bi, h, qi: (0, 0))
    qspec = pl.BlockSpec((1, tq, HEAD_W), lambda bi, h, qi: (bi, qi, h))
    kvspec = pl.BlockSpec((1, s, HEAD_W), lambda bi, h, qi: (bi, 0, h))
    return pl.pallas_call(
        functools.partial(_prompt_attn_kernel, tq=tq, n_chain=n_chain),
        grid=(b, N_HEADS, s // tq),
        in_specs=[lam_spec] * 4 + [pl.BlockSpec((1, HEAD_W), lambda bi, h, qi: (0, 0)),
                                   qspec, kvspec, kvspec],
        out_specs=qspec,
        out_shape=jax.ShapeDtypeStruct((b, s, GROUP_W), BF16),
        scratch_shapes=[pltpu.VMEM((HEAD_W, 2 * tq), BF16),
                        pltpu.VMEM((s // tq, HEAD_W + ONES_PAD, tq), BF16),
                        pltpu.VMEM((1, 2 * tq), F32),
                        pltpu.VMEM((HEAD_W + ONES_PAD, 2 * tq), F32),
                        pltpu.VMEM((2, n_chain, tq, 2 * tq // n_chain), F32)],
        compiler_params=_params(("arbitrary", "arbitrary", "arbitrary")),
        name="prompt_attn",
    )(*lams, g_subln, q, k, v)


def _sample_attn_kernel(pt_ref, lq1, lk1, lq2, lk2, g_ref, q_ref, kn_ref, vn_ref, *rest,
                        pages, t_new):
    k_refs = rest[:pages]
    v_refs = rest[pages:2 * pages]
    o_ref = rest[2 * pages]
    qbd_ref, m_ref, l_ref, acc_ref = rest[2 * pages + 1:]
    c = pl.program_id(1)
    nc = pl.num_programs(1)
    n_rows = qbd_ref.shape[0]

    @pl.when(c == 0)
    def _():
        q = q_ref[...].astype(F32)
        qt = jnp.concatenate([q] * (n_rows // t_new), axis=0)
        r = lax.broadcasted_iota(jnp.int32, qt.shape, 0)
        ln = lax.broadcasted_iota(jnp.int32, qt.shape, 1)
        qbd_ref[...] = jnp.where((r // t_new) == (ln // HALF_W), qt, 0.0).astype(BF16)
        m_ref[...] = jnp.full(m_ref.shape, -jnp.inf, F32)
        l_ref[...] = jnp.zeros(l_ref.shape, F32)
        acc_ref[...] = jnp.zeros(acc_ref.shape, F32)

    qbd = qbd_ref[...]

    def online(s_blocks, v_blocks):
        s = jnp.concatenate(s_blocks, axis=1) if len(s_blocks) > 1 else s_blocks[0]
        m_old = m_ref[...]
        m_new = jnp.maximum(m_old, jnp.max(s, axis=-1, keepdims=True))
        alpha = jnp.exp2(m_old - m_new)
        p = jnp.exp2(s - m_new)
        l_ref[...] = alpha * l_ref[...] + jnp.sum(p, axis=-1, keepdims=True)
        pv = None
        off = 0
        for vb in v_blocks:
            n = vb.shape[0]
            d = jnp.dot(p[:, off:off + n].astype(BF16), vb, preferred_element_type=F32)
            pv = d if pv is None else pv + d
            off += n
        acc_ref[...] = alpha * acc_ref[...] + pv
        m_ref[...] = m_new

    def scores(kb):
        return lax.dot_general(qbd, kb, (((1,), (1,)), ((), ())), preferred_element_type=F32)

    def load_page(ref):
        return jnp.concatenate([ref[pl.ds(h, PAGE, stride=N_HEADS), :] for h in range(N_HEADS)],
                               axis=1).astype(BF16)

    online([scores(load_page(r)) for r in k_refs], [load_page(r) for r in v_refs])

    @pl.when(c == nc - 1)
    def _():
        pad = jnp.zeros((PAGE - t_new, GROUP_W), F32)
        kn = jnp.concatenate([kn_ref[...], pad], axis=0).astype(BF16)
        vn = jnp.concatenate([vn_ref[...], pad], axis=0).astype(BF16)
        s = scores(kn)
        r = lax.broadcasted_iota(jnp.int32, s.shape, 0)
        col = lax.broadcasted_iota(jnp.int32, s.shape, 1)
        s = jnp.where(col <= (r % t_new), s, -jnp.inf)
        online([s], [vn])

        lam = _lam(lq1, lk1, lq2, lk2)
        o = acc_ref[...] / l_ref[...]
        for h in range(N_HEADS):
            blk = o[2 * t_new * h:2 * t_new * (h + 1), h * HEAD_W:(h + 1) * HEAD_W]
            o_ref[:, h * HEAD_W:(h + 1) * HEAD_W] = _subln(blk, lam, g_ref[...])


def _sample_attn(q, k_new, v_new, cache_k, cache_v, page_table, lams, g_subln, t_new, pages):
    n_b, n_pages = page_table.shape
    ck = cache_k.reshape(-1, HEAD_W)
    cv = cache_v.reshape(-1, HEAD_W)
    pt = page_table.reshape(-1)
    const = lambda shape: pl.BlockSpec(shape, lambda b, c, pt_ref: (0, 0))
    rows = pl.BlockSpec((t_new, GROUP_W), lambda b, c, pt_ref: (b, 0))

    def page_spec(j):
        return pl.BlockSpec(
            (PAGE * N_HEADS, HEAD_W),
            lambda b, c, pt_ref: (pt_ref[b * n_pages + c * pages + j], 0))

    grid_spec = pltpu.PrefetchScalarGridSpec(
        num_scalar_prefetch=1,
        grid=(n_b, n_pages // pages),
        in_specs=[const((1, HALF_W))] * 4 + [const((1, HEAD_W)), rows, rows, rows]
                 + [page_spec(j) for j in range(pages)] * 2,
        out_specs=rows,
        scratch_shapes=[pltpu.VMEM((2 * N_HEADS * t_new, GROUP_W), BF16),
                        pltpu.VMEM((2 * N_HEADS * t_new, 1), F32),
                        pltpu.VMEM((2 * N_HEADS * t_new, 1), F32),
                        pltpu.VMEM((2 * N_HEADS * t_new, GROUP_W), F32)])
    return pl.pallas_call(
        functools.partial(_sample_attn_kernel, pages=pages, t_new=t_new),
        grid_spec=grid_spec,
        out_shape=jax.ShapeDtypeStruct((n_b * t_new, GROUP_W), F32),
        compiler_params=_params(("arbitrary", "arbitrary")),
        name="sample_attn",
    )(pt, *lams, g_subln, q, k_new, v_new, *([ck] * pages), *([cv] * pages))


def _out_proj_kernel(a_ref, b_ref, w_ref, x_ref, gpost_ref, gpre_ref, h_ref, hn_ref):
    mix = jnp.dot(a_ref[...].astype(BF16), w_ref[:GROUP_W, :], preferred_element_type=F32)
    mix += jnp.dot(b_ref[...].astype(BF16), w_ref[GROUP_W:, :], preferred_element_type=F32)
    h = x_ref[...] + _rms(mix, gpost_ref[...])
    h_ref[...] = h
    hn_ref[...] = _rms(h, gpre_ref[...]).astype(hn_ref.dtype)


def _out_proj(a, b, w, x2d, g_post, g_pre, tm):
    m, d = x2d.shape
    row = lambda i: (i, 0)
    const = lambda i: (0, 0)
    return pl.pallas_call(
        _out_proj_kernel,
        grid=(m // tm,),
        in_specs=[pl.BlockSpec((tm, GROUP_W), row), pl.BlockSpec((tm, GROUP_W), row),
                  pl.BlockSpec(w.shape, const), pl.BlockSpec((tm, d), row),
                  pl.BlockSpec((1, d), const), pl.BlockSpec((1, d), const)],
        out_specs=[pl.BlockSpec((tm, d), row), pl.BlockSpec((tm, d), row)],
        out_shape=[jax.ShapeDtypeStruct((m, d), F32), jax.ShapeDtypeStruct((m, d), BF16)],
        compiler_params=_params(("arbitrary",)),
        name="out_proj",
    )(a, b, w, x2d, g_post, g_pre)


HALO = 16


def _gated(cg, cv):
    return (jax.nn.gelu(cg, approximate=True) * cv).astype(BF16)


def _ffn_finish(acc, h_ref, g_ref, o_ref):
    o_ref[...] = h_ref[...] + _rms(acc, g_ref[...])


def _prompt_ffn_kernel(hn_ref, halo_ref, wg_ref, wv_ref, cwg_ref, cwv_ref, cbg_ref, cbv_ref,
                       wd_ref, h_ref, g_ref, o_ref, lastg_ref, lastv_ref,
                       xh_ref, ug_ref, uv_ref, acc_ref, *, tm, tiles_per_seq):
    i = pl.program_id(0)
    f = pl.program_id(1)

    @pl.when(f == 0)
    def _():
        starts = (i % tiles_per_seq) == 0
        xh_ref[:HALO, :] = jnp.where(starts, jnp.zeros_like(halo_ref[...]), halo_ref[...])
        xh_ref[HALO:, :] = hn_ref[...]
        acc_ref[...] = jnp.zeros(acc_ref.shape, F32)

    xh = xh_ref[...]
    ug_ref[...] = jnp.dot(xh, wg_ref[...], preferred_element_type=F32)
    uv_ref[...] = jnp.dot(xh, wv_ref[...], preferred_element_type=F32)

    def conv(u_ref, cw_ref, cb_ref):
        out = cb_ref[...]
        for j in range(CONV_W):
            lo = HALO - (CONV_W - 1) + j
            out = out + u_ref[lo:lo + tm, :] * cw_ref[j:j + 1, :]
        return out

    act = _gated(conv(ug_ref, cwg_ref, cbg_ref), conv(uv_ref, cwv_ref, cbv_ref))
    acc_ref[...] += jnp.dot(act, wd_ref[...], preferred_element_type=F32)
    lastg_ref[0] = ug_ref[HALO + tm - (CONV_W - 1):, :]
    lastv_ref[0] = uv_ref[HALO + tm - (CONV_W - 1):, :]

    @pl.when(f == pl.num_programs(1) - 1)
    def _():
        _ffn_finish(acc_ref[...], h_ref, g_ref, o_ref)


def _prompt_ffn(hn, h, w_up, w_conv, b_conv, w_down, g_post, tm, tf, seq):
    m, d = h.shape
    d_ff = w_down.shape[0]
    nf = d_ff // tf
    nt = m // tm
    row = lambda i, f: (i, 0)
    const = lambda i, f: (0, 0)
    gate = lambda i, f: (0, f)
    val = lambda i, f: (0, nf + f)
    last = pl.BlockSpec((1, CONV_W - 1, tf), lambda i, f: (i, 0, f))
    return pl.pallas_call(
        functools.partial(_prompt_ffn_kernel, tm=tm, tiles_per_seq=seq // tm),
        grid=(nt, nf),
        in_specs=[pl.BlockSpec((tm, d), row),
                  pl.BlockSpec((HALO, d), lambda i, f: (jnp.maximum(i * (tm // HALO) - 1, 0), 0)),
                  pl.BlockSpec((d, tf), gate), pl.BlockSpec((d, tf), val),
                  pl.BlockSpec((CONV_W, tf), gate), pl.BlockSpec((CONV_W, tf), val),
                  pl.BlockSpec((1, tf), gate), pl.BlockSpec((1, tf), val),
                  pl.BlockSpec((tf, d), lambda i, f: (f, 0)),
                  pl.BlockSpec((tm, d), row),
                  pl.BlockSpec((1, d), const)],
        out_specs=[pl.BlockSpec((tm, d), row), last, last],
        out_shape=[jax.ShapeDtypeStruct((m, d), F32),
                   jax.ShapeDtypeStruct((nt, CONV_W - 1, d_ff), F32),
                   jax.ShapeDtypeStruct((nt, CONV_W - 1, d_ff), F32)],
        scratch_shapes=[pltpu.VMEM((tm + HALO, d), BF16),
                        pltpu.VMEM((tm + HALO, tf), F32),
                        pltpu.VMEM((tm + HALO, tf), F32),
                        pltpu.VMEM((tm, d), F32)],
        compiler_params=_params(("arbitrary", "arbitrary")),
        name="prompt_ffn",
    )(hn, hn, w_up, w_up, w_conv, w_conv, b_conv, b_conv, w_down, h, g_post)


def _sample_ffn_kernel(hn_ref, histg_ref, histv_ref, wg_ref, wv_ref, cwg_ref, cwv_ref,
                       cbg_ref, cbv_ref, wd_ref, h_ref, g_ref, o_ref, newg_ref, newv_ref,
                       acc_ref, hpg_ref, hpv_ref, *, n_b, t_new):
    f = pl.program_id(0)
    nh = CONV_W - 1

    @pl.when(f == 0)
    def _():
        acc_ref[...] = jnp.zeros(acc_ref.shape, F32)

    hn = hn_ref[...]
    ug = jnp.dot(hn, wg_ref[...], preferred_element_type=F32)
    uv = jnp.dot(hn, wv_ref[...], preferred_element_type=F32)

    def conv(u, hist_ref, cw_ref, cb_ref, new_ref, hp_ref):
        outs = []
        for b in range(n_b):
            hp_ref[b, t_new - nh:t_new, :] = hist_ref[b]
            hp_ref[b, t_new:, :] = u[b * t_new:(b + 1) * t_new, :]
            c = cb_ref[...]
            for j in range(CONV_W):
                lo = t_new - nh + j
                c = c + hp_ref[b, lo:lo + t_new, :] * cw_ref[j:j + 1, :]
            outs.append(c)
            new_ref[b] = hp_ref[b, 2 * t_new - nh:, :]
        return jnp.concatenate(outs, axis=0)

    act = _gated(conv(ug, histg_ref, cwg_ref, cbg_ref, newg_ref, hpg_ref),
                 conv(uv, histv_ref, cwv_ref, cbv_ref, newv_ref, hpv_ref))
    acc_ref[...] += jnp.dot(act, wd_ref[...], preferred_element_type=F32)

    @pl.when(f == pl.num_programs(0) - 1)
    def _():
        _ffn_finish(acc_ref[...], h_ref, g_ref, o_ref)


def _sample_ffn(hn, h, hist, w_up, w_conv, b_conv, w_down, g_post, tf, n_b, t_new):
    m, d = h.shape
    d_ff = w_down.shape[0]
    nf = d_ff // tf
    const = lambda f: (0, 0)
    gate = lambda f: (0, f)
    val = lambda f: (0, nf + f)
    hist_g = pl.BlockSpec((n_b, CONV_W - 1, tf), lambda f: (0, 0, f))
    hist_v = pl.BlockSpec((n_b, CONV_W - 1, tf), lambda f: (0, 0, nf + f))
    return pl.pallas_call(
        functools.partial(_sample_ffn_kernel, n_b=n_b, t_new=t_new),
        grid=(nf,),
        in_specs=[pl.BlockSpec((m, d), const), hist_g, hist_v,
                  pl.BlockSpec((d, tf), gate), pl.BlockSpec((d, tf), val),
                  pl.BlockSpec((CONV_W, tf), gate), pl.BlockSpec((CONV_W, tf), val),
                  pl.BlockSpec((1, tf), gate), pl.BlockSpec((1, tf), val),
                  pl.BlockSpec((tf, d), lambda f: (f, 0)),
                  pl.BlockSpec((m, d), const), pl.BlockSpec((1, d), const)],
        out_specs=[pl.BlockSpec((m, d), const), hist_g, hist_g],
        out_shape=[jax.ShapeDtypeStruct((m, d), F32),
                   jax.ShapeDtypeStruct((n_b, CONV_W - 1, d_ff), F32),
                   jax.ShapeDtypeStruct((n_b, CONV_W - 1, d_ff), F32)],
        scratch_shapes=[pltpu.VMEM((m, d), F32),
                        pltpu.VMEM((n_b, 2 * t_new, tf), F32),
                        pltpu.VMEM((n_b, 2 * t_new, tf), F32)],
        compiler_params=_params(("arbitrary",)),
        name="sample_ffn",
    )(hn, hist, hist, w_up, w_up, w_conv, w_conv, b_conv, b_conv, w_down, h, g_post)


def kernel(x_prompt, x_sample, cache_k, cache_v, state_conv, page_table, g_pre_mix, w_in, g_v_a,
           w_s, b_s, lam_q1, lam_k1, lam_q2, lam_k2, g_subln, w_out, g_post_mix, g_pre_ffn, w_up,
           w_conv, b_conv, w_down, g_post_ffn):
    n_bp, seq, d = x_prompt.shape
    n_bs, t_new, _ = x_sample.shape
    depth = w_in.shape[0]
    assert depth == 1
    d_ff = w_down.shape[1]

    w_in_b = w_in[0].astype(BF16)
    w_out_b = w_out[0].astype(BF16)
    w_up_b = w_up[0].astype(BF16)
    w_down_b = w_down[0].astype(BF16)
    b_t = b_s[0].T
    lams = (lam_q1, lam_k1, lam_q2, lam_k2)
    b_conv2 = b_conv

    xp = x_prompt.reshape(n_bp * seq, d)
    u, va, q, k, v, kb, vb = _in_proj(xp, g_pre_mix, w_in_b, g_v_a[0], 512, BF16)
    a_out = _gmlp(u, va, w_s[0], b_t, CHUNK, 512, BF16)
    sh3 = (n_bp, seq, GROUP_W)
    b_out = _prompt_attn(q.reshape(sh3), kb.reshape(sh3), vb.reshape(sh3), lams, g_subln, 512, 4)
    h, hn = _out_proj(a_out, b_out.reshape(n_bp * seq, GROUP_W), w_out_b, xp,
                      g_post_mix, g_pre_ffn, 512)
    tm_ffn = 512
    yp, lastg, lastv = _prompt_ffn(hn, h, w_up_b, w_conv[0], b_conv2, w_down_b, g_post_ffn,
                                   tm_ffn, 512, seq)
    tps = seq // tm_ffn
    conv_p = jnp.concatenate([lastg[tps - 1::tps], lastv[tps - 1::tps]], axis=-1)

    xs = x_sample.reshape(n_bs * t_new, d)
    us, vas, qs, ks, vs, _, _ = _in_proj(xs, g_pre_mix, w_in_b, g_v_a[0], n_bs * t_new, F32)
    a_s = _gmlp(us, vas, w_s[0], b_t, t_new, n_bs * t_new, F32)
    b_sm = _sample_attn(qs, ks, vs, cache_k, cache_v, page_table, lams, g_subln, t_new, 8)
    hs, hns = _out_proj(a_s, b_sm, w_out_b, xs, g_post_mix, g_pre_ffn, n_bs * t_new)
    ys, newg, newv = _sample_ffn(hns, hs, state_conv[0], w_up_b, w_conv[0], b_conv2, w_down_b,
                                 g_post_ffn, 512, n_bs, t_new)
    conv_s = jnp.concatenate([newg, newv], axis=-1)

    hsh = (depth, n_bp, seq, N_HEADS, HEAD_W)
    ssh = (depth, n_bs, t_new, N_HEADS, HEAD_W)
    return (yp.reshape(n_bp, seq, d), ys.reshape(n_bs, t_new, d),
            k.reshape(hsh), v.reshape(hsh), conv_p[None],
            ks.reshape(ssh), vs.reshape(ssh), conv_s[None], vas.reshape(ssh))
```

```python
import functools
import math

import jax
import jax.numpy as jnp
from jax import lax
from jax.experimental import pallas as pl
from jax.experimental.pallas import tpu as pltpu

F32 = jnp.float32
BF16 = jnp.bfloat16

EPS = 1e-6
HEAD_W = 128
HALF_W = HEAD_W // 2
N_HEADS = 8
GROUP_W = N_HEADS * HEAD_W
CHUNK = 128
PAGE = 128
CONV_W = 3
LAM_INIT = 0.8 - 0.6 * math.exp(0.0)
Q_SCALE = HALF_W ** -0.5 * math.log2(math.e)
VMEM_LIMIT = 56 * 1024 * 1024


def _params(sem):
    return pltpu.CompilerParams(dimension_semantics=sem, vmem_limit_bytes=VMEM_LIMIT)


def _rms(x, g):
    return x * lax.rsqrt(jnp.mean(x * x, axis=-1, keepdims=True) + EPS) * g


def _lam(lq1, lk1, lq2, lk2):
    a = jnp.sum(lq1[...] * lk1[...], axis=-1, keepdims=True)
    b = jnp.sum(lq2[...] * lk2[...], axis=-1, keepdims=True)
    return jnp.exp(a) - jnp.exp(b) + LAM_INIT


MXU_N = 512


def _tril_weights(ws_ref, h):
    r = lax.broadcasted_iota(jnp.int32, (CHUNK, CHUNK), 0)
    c = lax.broadcasted_iota(jnp.int32, (CHUNK, CHUNK), 1)
    return jnp.where(c <= r, ws_ref[h], 0.0).astype(BF16)


def _in_proj_kernel(x_ref, g_ref, w_ref, gva_ref, ws_ref, bt_ref, *rest, rows, emit_va, cast_w):
    rest = list(rest)
    a_ref = rest.pop(0)
    va_ref = rest.pop(0) if emit_va else None
    q_ref, k_ref, v_ref, kb_ref, vb_ref = rest[:5]
    rest = rest[5:]
    wb_ref = rest.pop(0) if cast_w else w_ref
    xg_ref, r_ref, u_ref = rest
    j = pl.program_id(1)
    tm = xg_ref.shape[0]
    hpc = MXU_N // HEAD_W

    if cast_w:
        wb_ref[...] = w_ref[...].astype(BF16)

    def project(c):
        z = jnp.dot(xg_ref[...], wb_ref[:, c * MXU_N:(c + 1) * MXU_N], preferred_element_type=F32)
        return z * r_ref[...]

    def chunks():
        return [(c, slice(c * MXU_N, (c + 1) * MXU_N)) for c in range(GROUP_W // MXU_N)]

    @pl.when(j == 0)
    def _():
        x = x_ref[...]
        xg_ref[...] = (x * g_ref[...]).astype(BF16)
        r_ref[...] = lax.rsqrt(jnp.mean(x * x, axis=-1, keepdims=True) + EPS)
        for c, cs in chunks():
            u_ref[:, cs] = jax.nn.gelu(project(c), approximate=True).astype(u_ref.dtype)

    @pl.when(j == 1)
    def _():
        for c, _ in chunks():
            a = jax.nn.gelu(project(c), approximate=True)
            for hh in range(hpc):
                h = c * hpc + hh
                sl = slice(h * HEAD_W, (h + 1) * HEAD_W)
                va = _rms(a[:, hh * HEAD_W:(hh + 1) * HEAD_W], gva_ref[h:h + 1, :])
                if emit_va:
                    va_ref[:, sl] = va
                w = _tril_weights(ws_ref, h)
                bias = bt_ref[:, h:h + 1]
                for t in range(tm // rows):
                    rs = slice(t * rows, (t + 1) * rows)
                    vc = va[rs]
                    if rows < CHUNK:
                        vc = jnp.concatenate([vc, jnp.zeros((CHUNK - rows, HEAD_W), F32)], axis=0)
                    mixed = jnp.dot(w, vc.astype(BF16), preferred_element_type=F32) + bias
                    a_ref[rs, sl] = (u_ref[rs, sl].astype(F32) * mixed[:rows]).astype(a_ref.dtype)

    @pl.when(j == 2)
    def _():
        for c, cs in chunks():
            q_ref[:, cs] = (project(c) * Q_SCALE).astype(q_ref.dtype)

    @pl.when(j == 3)
    def _():
        for c, cs in chunks():
            z = project(c)
            k_ref[:, cs] = z
            kb_ref[:, cs] = z.astype(BF16)

    @pl.when(j == 4)
    def _():
        for c, cs in chunks():
            z = project(c)
            v_ref[:, cs] = z
            vb_ref[:, cs] = z.astype(BF16)


def _in_proj(x2d, g, w, gva, w_s, b_t, tm, rows, act_dtype, emit_va, cast_w):
    m, d = x2d.shape
    row = lambda i, j: (i, 0)
    const = lambda i, j: (0, 0)
    blk = pl.BlockSpec((tm, GROUP_W), row)
    wspec = pl.BlockSpec((d, GROUP_W), lambda i, j: (0, j))
    act = jax.ShapeDtypeStruct((m, GROUP_W), act_dtype)
    out_shape = ([act] + ([jax.ShapeDtypeStruct((m, GROUP_W), F32)] if emit_va else [])
                 + [act, jax.ShapeDtypeStruct((m, GROUP_W), F32),
                    jax.ShapeDtypeStruct((m, GROUP_W), F32),
                    jax.ShapeDtypeStruct((m, GROUP_W), BF16),
                    jax.ShapeDtypeStruct((m, GROUP_W), BF16)])
    out_specs = [blk] * len(out_shape)
    if cast_w:
        assert m == tm, "the bf16 weight copy is written once per column group"
        out_shape.append(jax.ShapeDtypeStruct(w.shape, BF16))
        out_specs.append(wspec)
    return pl.pallas_call(
        functools.partial(_in_proj_kernel, rows=rows, emit_va=emit_va, cast_w=cast_w),
        grid=(m // tm, 5),
        in_specs=[pl.BlockSpec((tm, d), row), pl.BlockSpec((1, d), const), wspec,
                  pl.BlockSpec((N_HEADS, HEAD_W), const),
                  pl.BlockSpec((N_HEADS, CHUNK, CHUNK), lambda i, j: (0, 0, 0)),
                  pl.BlockSpec((CHUNK, N_HEADS), const)],
        out_specs=out_specs,
        out_shape=out_shape,
        scratch_shapes=[pltpu.VMEM((tm, d), BF16), pltpu.VMEM((tm, 1), F32),
                        pltpu.VMEM((tm, GROUP_W), act_dtype)],
        compiler_params=_params(("arbitrary", "arbitrary")),
        name="in_proj",
    )(x2d, g, w, gva, w_s, b_t)


ONES_PAD = 16


def _subln(o, lam, g):
    t = o.shape[0] // 2
    d = o[:t] - lam * o[t:]
    return _rms(d, g) * (1.0 - LAM_INIT)


def _prompt_attn_kernel(lq1, lk1, lq2, lk2, g_ref, q_ref, k_ref, v_ref, o_ref,
                        qt_ref, vt_ref, m_ref, acc_ref, s_ref, *, tq, n_chain):
    qi = pl.program_id(2)

    @pl.when(qi == 0)
    def _():
        ones_row = lax.broadcasted_iota(jnp.int32, (ONES_PAD, tq), 0) == 0
        for c in range(vt_ref.shape[0]):
            vt_ref[c, :HEAD_W, :] = v_ref[0, c * tq:(c + 1) * tq, :].astype(F32).T.astype(BF16)
            vt_ref[c, HEAD_W:, :] = jnp.where(ones_row, 1.0, 0.0).astype(BF16)

    qf = q_ref[0].astype(F32).T
    dim = lax.broadcasted_iota(jnp.int32, qf.shape, 0)
    qt_ref[:, :tq] = jnp.where(dim < HALF_W, qf, 0.0).astype(BF16)
    qt_ref[:, tq:] = jnp.where(dim >= HALF_W, qf, 0.0).astype(BF16)
    m_ref[...] = jnp.full(m_ref.shape, -jnp.inf, F32)
    acc_ref[...] = jnp.zeros(acc_ref.shape, F32)
    cw = 2 * tq // n_chain

    def scores(kb, slot, masked):
        k = k_ref[0, pl.ds(pl.multiple_of(kb * tq, tq), tq), :]
        for c in range(n_chain):
            cs = slice(c * cw, (c + 1) * cw)
            s = jnp.dot(k, qt_ref[:, cs], preferred_element_type=F32)
            if masked:
                key = lax.broadcasted_iota(jnp.int32, s.shape, 0)
                qry = lax.broadcasted_iota(jnp.int32, s.shape, 1) + (c * cw) % tq
                s = jnp.where(key <= qry, s, -jnp.inf)
            s_ref[slot, c] = s

    def absorb(kb, slot):
        vt = vt_ref[kb]
        for c in range(n_chain):
            cs = slice(c * cw, (c + 1) * cw)
            s = s_ref[slot, c]
            m_old = m_ref[:, cs]
            m_new = jnp.maximum(m_old, jnp.max(s, axis=0, keepdims=True))
            alpha = jnp.exp2(m_old - m_new)
            p = jnp.exp2(s - m_new)
            acc_ref[:, cs] = alpha * acc_ref[:, cs] + jnp.dot(vt, p.astype(BF16),
                                                              preferred_element_type=F32)
            m_ref[:, cs] = m_new

    @pl.when(qi == 0)
    def _():
        scores(0, 0, True)

    @pl.when(qi > 0)
    def _():
        scores(0, 0, False)

    n_pair = lax.shift_right_logical(jnp.maximum(qi - 1, 0), 1)

    def pair(t, carry):
        kb = 2 * t
        scores(kb + 1, 1, False)
        absorb(kb, 0)
        scores(kb + 2, 0, False)
        absorb(kb + 1, 1)
        return carry

    lax.fori_loop(0, n_pair, pair, 0)
    left = qi - 2 * n_pair

    @pl.when(left == 0)
    def _():
        absorb(qi, 0)

    @pl.when(left == 1)
    def _():
        scores(qi, 1, True)
        absorb(qi - 1, 0)
        absorb(qi, 1)

    @pl.when(left == 2)
    def _():
        scores(qi - 1, 1, False)
        absorb(qi - 2, 0)
        scores(qi, 0, True)
        absorb(qi - 1, 1)
        absorb(qi, 0)

    lam = _lam(lq1, lk1, lq2, lk2)
    o = acc_ref[:HEAD_W, :] / acc_ref[HEAD_W:HEAD_W + 1, :]
    d = o[:, :tq] - lam * o[:, tq:]
    dn = d * lax.rsqrt(jnp.mean(d * d, axis=0, keepdims=True) + EPS)
    o_ref[0] = (dn.T * g_ref[...] * (1.0 - LAM_INIT)).astype(o_ref.dtype)


def _prompt_attn(q, k, v, lams, g_subln, tq, n_chain):
    b, s, _ = q.shape
    lam_spec = pl.BlockSpec((1, HALF_W), lambda bi, h, qi: (0, 0))
    qspec = pl.BlockSpec((1, tq, HEAD_W), lambda bi, h, qi: (bi, qi, h))
    kvspec = pl.BlockSpec((1, s, HEAD_W), lambda bi, h, qi: (bi, 0, h))
    return pl.pallas_call(
        functools.partial(_prompt_attn_kernel, tq=tq, n_chain=n_chain),
        grid=(b, N_HEADS, s // tq),
        in_specs=[lam_spec] * 4 + [pl.BlockSpec((1, HEAD_W), lambda bi, h, qi: (0, 0)),
                                   qspec, kvspec, kvspec],
        out_specs=qspec,
        out_shape=jax.ShapeDtypeStruct((b, s, GROUP_W), BF16),
        scratch_shapes=[pltpu.VMEM((HEAD_W, 2 * tq), BF16),
                        pltpu.VMEM((s // tq, HEAD_W + ONES_PAD, tq), BF16),
                        pltpu.VMEM((1, 2 * tq), F32),
                        pltpu.VMEM((HEAD_W + ONES_PAD, 2 * tq), F32),
                        pltpu.VMEM((2, n_chain, tq, 2 * tq // n_chain), F32)],
        compiler_params=_params(("arbitrary", "arbitrary", "arbitrary")),
        name="prompt_attn",
    )(*lams, g_subln, q, k, v)


def _sample_attn_kernel(pt_ref, lq1, lk1, lq2, lk2, g_ref, q_ref, kn_ref, vn_ref, *rest,
                        pages, t_new):
    k_refs = rest[:pages]
    v_refs = rest[pages:2 * pages]
    o_ref = rest[2 * pages]
    qbd_ref, m_ref, l_ref, acc_ref = rest[2 * pages + 1:]
    c = pl.program_id(1)
    nc = pl.num_programs(1)
    n_rows = qbd_ref.shape[0]

    @pl.when(c == 0)
    def _():
        q = q_ref[...].astype(F32)
        qt = jnp.concatenate([q] * (n_rows // t_new), axis=0)
        r = lax.broadcasted_iota(jnp.int32, qt.shape, 0)
        ln = lax.broadcasted_iota(jnp.int32, qt.shape, 1)
        qbd_ref[...] = jnp.where((r // t_new) == (ln // HALF_W), qt, 0.0).astype(BF16)
        m_ref[...] = jnp.full(m_ref.shape, -jnp.inf, F32)
        l_ref[...] = jnp.zeros(l_ref.shape, F32)
        acc_ref[...] = jnp.zeros(acc_ref.shape, F32)

    qbd = qbd_ref[...]

    def online(s_blocks, v_blocks):
        s = jnp.concatenate(s_blocks, axis=1) if len(s_blocks) > 1 else s_blocks[0]
        m_old = m_ref[...]
        m_new = jnp.maximum(m_old, jnp.max(s, axis=-1, keepdims=True))
        alpha = jnp.exp2(m_old - m_new)
        p = jnp.exp2(s - m_new)
        l_ref[...] = alpha * l_ref[...] + jnp.sum(p, axis=-1, keepdims=True)
        pv = None
        off = 0
        for vb in v_blocks:
            n = vb.shape[0]
            d = jnp.dot(p[:, off:off + n].astype(BF16), vb, preferred_element_type=F32)
            pv = d if pv is None else pv + d
            off += n
        acc_ref[...] = alpha * acc_ref[...] + pv
        m_ref[...] = m_new

    def scores(kb):
        return lax.dot_general(qbd, kb, (((1,), (1,)), ((), ())), preferred_element_type=F32)

    def load_page(ref):
        return jnp.concatenate([ref[pl.ds(h, PAGE, stride=N_HEADS), :] for h in range(N_HEADS)],
                               axis=1).astype(BF16)

    online([scores(load_page(r)) for r in k_refs], [load_page(r) for r in v_refs])

    @pl.when(c == nc - 1)
    def _():
        pad = jnp.zeros((PAGE - t_new, GROUP_W), F32)
        kn = jnp.concatenate([kn_ref[...], pad], axis=0).astype(BF16)
        vn = jnp.concatenate([vn_ref[...], pad], axis=0).astype(BF16)
        s = scores(kn)
        r = lax.broadcasted_iota(jnp.int32, s.shape, 0)
        col = lax.broadcasted_iota(jnp.int32, s.shape, 1)
        s = jnp.where(col <= (r % t_new), s, -jnp.inf)
        online([s], [vn])

        lam = _lam(lq1, lk1, lq2, lk2)
        o = acc_ref[...] / l_ref[...]
        for h in range(N_HEADS):
            blk = o[2 * t_new * h:2 * t_new * (h + 1), h * HEAD_W:(h + 1) * HEAD_W]
            o_ref[:, h * HEAD_W:(h + 1) * HEAD_W] = _subln(blk, lam, g_ref[...])


def _sample_attn(q, k_new, v_new, cache_k, cache_v, page_table, lams, g_subln, t_new, pages):
    n_b, n_pages = page_table.shape
    ck = cache_k.reshape(-1, HEAD_W)
    cv = cache_v.reshape(-1, HEAD_W)
    pt = page_table.reshape(-1)
    const = lambda shape: pl.BlockSpec(shape, lambda b, c, pt_ref: (0, 0))
    rows = pl.BlockSpec((t_new, GROUP_W), lambda b, c, pt_ref: (b, 0))

    def page_spec(j):
        return pl.BlockSpec(
            (PAGE * N_HEADS, HEAD_W),
            lambda b, c, pt_ref: (pt_ref[b * n_pages + c * pages + j], 0))

    grid_spec = pltpu.PrefetchScalarGridSpec(
        num_scalar_prefetch=1,
        grid=(n_b, n_pages // pages),
        in_specs=[const((1, HALF_W))] * 4 + [const((1, HEAD_W)), rows, rows, rows]
                 + [page_spec(j) for j in range(pages)] * 2,
        out_specs=rows,
        scratch_shapes=[pltpu.VMEM((2 * N_HEADS * t_new, GROUP_W), BF16),
                        pltpu.VMEM((2 * N_HEADS * t_new, 1), F32),
                        pltpu.VMEM((2 * N_HEADS * t_new, 1), F32),
                        pltpu.VMEM((2 * N_HEADS * t_new, GROUP_W), F32)])
    return pl.pallas_call(
        functools.partial(_sample_attn_kernel, pages=pages, t_new=t_new),
        grid_spec=grid_spec,
        out_shape=jax.ShapeDtypeStruct((n_b * t_new, GROUP_W), F32),
        compiler_params=_params(("arbitrary", "arbitrary")),
        name="sample_attn",
    )(pt, *lams, g_subln, q, k_new, v_new, *([ck] * pages), *([cv] * pages))


def _out_proj_kernel(a_ref, b_ref, w_ref, x_ref, gpost_ref, gpre_ref, h_ref, hn_ref, *wb_refs):
    if wb_refs:
        wb_ref, = wb_refs
        wb_ref[...] = w_ref[...].astype(BF16)
    else:
        wb_ref = w_ref
    mix = jnp.dot(a_ref[...].astype(BF16), wb_ref[:GROUP_W, :], preferred_element_type=F32)
    mix += jnp.dot(b_ref[...].astype(BF16), wb_ref[GROUP_W:, :], preferred_element_type=F32)
    h = x_ref[...] + _rms(mix, gpost_ref[...])
    h_ref[...] = h
    hn_ref[...] = _rms(h, gpre_ref[...]).astype(hn_ref.dtype)


def _out_proj(a, b, w, x2d, g_post, g_pre, tm, cast_w):
    m, d = x2d.shape
    row = lambda i: (i, 0)
    const = lambda i: (0, 0)
    out_specs = [pl.BlockSpec((tm, d), row), pl.BlockSpec((tm, d), row)]
    out_shape = [jax.ShapeDtypeStruct((m, d), F32), jax.ShapeDtypeStruct((m, d), BF16)]
    if cast_w:
        assert m == tm, "the bf16 weight copy is written once"
        out_specs.append(pl.BlockSpec(w.shape, const))
        out_shape.append(jax.ShapeDtypeStruct(w.shape, BF16))
    return pl.pallas_call(
        _out_proj_kernel,
        grid=(m // tm,),
        in_specs=[pl.BlockSpec((tm, GROUP_W), row), pl.BlockSpec((tm, GROUP_W), row),
                  pl.BlockSpec(w.shape, const), pl.BlockSpec((tm, d), row),
                  pl.BlockSpec((1, d), const), pl.BlockSpec((1, d), const)],
        out_specs=out_specs,
        out_shape=out_shape,
        compiler_params=_params(("arbitrary",)),
        name="out_proj",
    )(a, b, w, x2d, g_post, g_pre)


HALO = 16


def _gated(cg, cv):
    return (jax.nn.gelu(cg, approximate=True) * cv).astype(BF16)


def _ffn_finish(acc, h_ref, g_ref, o_ref):
    o_ref[...] = h_ref[...] + _rms(acc, g_ref[...])


def _prompt_ffn_kernel(hn_ref, halo_ref, wg_ref, wv_ref, cwg_ref, cwv_ref, cbg_ref, cbv_ref,
                       wd_ref, h_ref, g_ref, o_ref, lastg_ref, lastv_ref,
                       xh_ref, ug_ref, uv_ref, acc_ref, *, tm, tiles_per_seq):
    i = pl.program_id(0)
    f = pl.program_id(1)

    @pl.when(f == 0)
    def _():
        starts = (i % tiles_per_seq) == 0
        xh_ref[:HALO, :] = jnp.where(starts, jnp.zeros_like(halo_ref[...]), halo_ref[...])
        xh_ref[HALO:, :] = hn_ref[...]
        acc_ref[...] = jnp.zeros(acc_ref.shape, F32)

    xh = xh_ref[...]
    ug_ref[...] = jnp.dot(xh, wg_ref[...], preferred_element_type=F32)
    uv_ref[...] = jnp.dot(xh, wv_ref[...], preferred_element_type=F32)

    def conv(u_ref, cw_ref, cb_ref):
        out = cb_ref[...]
        for j in range(CONV_W):
            lo = HALO - (CONV_W - 1) + j
            out = out + u_ref[lo:lo + tm, :] * cw_ref[j:j + 1, :]
        return out

    act = _gated(conv(ug_ref, cwg_ref, cbg_ref), conv(uv_ref, cwv_ref, cbv_ref))
    acc_ref[...] += jnp.dot(act, wd_ref[...], preferred_element_type=F32)
    lastg_ref[0] = ug_ref[HALO + tm - (CONV_W - 1):, :]
    lastv_ref[0] = uv_ref[HALO + tm - (CONV_W - 1):, :]

    @pl.when(f == pl.num_programs(1) - 1)
    def _():
        _ffn_finish(acc_ref[...], h_ref, g_ref, o_ref)


def _prompt_ffn(hn, h, w_gate, w_val, w_conv, b_conv, w_down, g_post, tm, tf, seq):
    m, d = h.shape
    d_ff = w_down.shape[0]
    nf = d_ff // tf
    nt = m // tm
    row = lambda i, f: (i, 0)
    const = lambda i, f: (0, 0)
    gate = lambda i, f: (0, f)
    val = lambda i, f: (0, nf + f)
    last = pl.BlockSpec((1, CONV_W - 1, tf), lambda i, f: (i, 0, f))
    return pl.pallas_call(
        functools.partial(_prompt_ffn_kernel, tm=tm, tiles_per_seq=seq // tm),
        grid=(nt, nf),
        in_specs=[pl.BlockSpec((tm, d), row),
                  pl.BlockSpec((HALO, d), lambda i, f: (jnp.maximum(i * (tm // HALO) - 1, 0), 0)),
                  pl.BlockSpec((d, tf), gate), pl.BlockSpec((d, tf), gate),
                  pl.BlockSpec((CONV_W, tf), gate), pl.BlockSpec((CONV_W, tf), val),
                  pl.BlockSpec((1, tf), gate), pl.BlockSpec((1, tf), val),
                  pl.BlockSpec((tf, d), lambda i, f: (f, 0)),
                  pl.BlockSpec((tm, d), row),
                  pl.BlockSpec((1, d), const)],
        out_specs=[pl.BlockSpec((tm, d), row), last, last],
        out_shape=[jax.ShapeDtypeStruct((m, d), F32),
                   jax.ShapeDtypeStruct((nt, CONV_W - 1, d_ff), F32),
                   jax.ShapeDtypeStruct((nt, CONV_W - 1, d_ff), F32)],
        scratch_shapes=[pltpu.VMEM((tm + HALO, d), BF16),
                        pltpu.VMEM((tm + HALO, tf), F32),
                        pltpu.VMEM((tm + HALO, tf), F32),
                        pltpu.VMEM((tm, d), F32)],
        compiler_params=_params(("arbitrary", "arbitrary")),
        name="prompt_ffn",
    )(hn, hn, w_gate, w_val, w_conv, w_conv, b_conv, b_conv, w_down, h, g_post)


def _sample_ffn_kernel(hn_ref, histg_ref, histv_ref, wg_ref, wv_ref, cwg_ref, cwv_ref,
                       cbg_ref, cbv_ref, wd_ref, h_ref, g_ref, o_ref, newg_ref, newv_ref,
                       wgb_ref, wvb_ref, wdb_ref, acc_ref, hpg_ref, hpv_ref, *, n_b, t_new):
    f = pl.program_id(0)
    nh = CONV_W - 1

    @pl.when(f == 0)
    def _():
        acc_ref[...] = jnp.zeros(acc_ref.shape, F32)

    wgb_ref[...] = wg_ref[...].astype(BF16)
    wvb_ref[...] = wv_ref[...].astype(BF16)
    wdb_ref[...] = wd_ref[...].astype(BF16)
    hn = hn_ref[...]
    ug = jnp.dot(hn, wgb_ref[...], preferred_element_type=F32)
    uv = jnp.dot(hn, wvb_ref[...], preferred_element_type=F32)

    def conv(u, hist_ref, cw_ref, cb_ref, new_ref, hp_ref):
        outs = []
        for b in range(n_b):
            hp_ref[b, t_new - nh:t_new, :] = hist_ref[b]
            hp_ref[b, t_new:, :] = u[b * t_new:(b + 1) * t_new, :]
            c = cb_ref[...]
            for j in range(CONV_W):
                lo = t_new - nh + j
                c = c + hp_ref[b, lo:lo + t_new, :] * cw_ref[j:j + 1, :]
            outs.append(c)
            new_ref[b] = hp_ref[b, 2 * t_new - nh:, :]
        return jnp.concatenate(outs, axis=0)

    act = _gated(conv(ug, histg_ref, cwg_ref, cbg_ref, newg_ref, hpg_ref),
                 conv(uv, histv_ref, cwv_ref, cbv_ref, newv_ref, hpv_ref))
    acc_ref[...] += jnp.dot(act, wdb_ref[...], preferred_element_type=F32)

    @pl.when(f == pl.num_programs(0) - 1)
    def _():
        _ffn_finish(acc_ref[...], h_ref, g_ref, o_ref)


def _sample_ffn(hn, h, hist, w_up, w_conv, b_conv, w_down, g_post, tf, n_b, t_new):
    m, d = h.shape
    d_ff = w_down.shape[0]
    nf = d_ff // tf
    const = lambda f: (0, 0)
    gate = lambda f: (0, f)
    val = lambda f: (0, nf + f)
    hist_g = pl.BlockSpec((n_b, CONV_W - 1, tf), lambda f: (0, 0, f))
    hist_v = pl.BlockSpec((n_b, CONV_W - 1, tf), lambda f: (0, 0, nf + f))
    return pl.pallas_call(
        functools.partial(_sample_ffn_kernel, n_b=n_b, t_new=t_new),
        grid=(nf,),
        in_specs=[pl.BlockSpec((m, d), const), hist_g, hist_v,
                  pl.BlockSpec((d, tf), gate), pl.BlockSpec((d, tf), val),
                  pl.BlockSpec((CONV_W, tf), gate), pl.BlockSpec((CONV_W, tf), val),
                  pl.BlockSpec((1, tf), gate), pl.BlockSpec((1, tf), val),
                  pl.BlockSpec((tf, d), lambda f: (f, 0)),
                  pl.BlockSpec((m, d), const), pl.BlockSpec((1, d), const)],
        out_specs=[pl.BlockSpec((m, d), const), hist_g, hist_g,
                   pl.BlockSpec((d, tf), gate), pl.BlockSpec((d, tf), gate),
                   pl.BlockSpec((tf, d), lambda f: (f, 0))],
        out_shape=[jax.ShapeDtypeStruct((m, d), F32),
                   jax.ShapeDtypeStruct((n_b, CONV_W - 1, d_ff), F32),
                   jax.ShapeDtypeStruct((n_b, CONV_W - 1, d_ff), F32),
                   jax.ShapeDtypeStruct((d, d_ff), BF16),
                   jax.ShapeDtypeStruct((d, d_ff), BF16),
                   jax.ShapeDtypeStruct((d_ff, d), BF16)],
        scratch_shapes=[pltpu.VMEM((m, d), F32),
                        pltpu.VMEM((n_b, 2 * t_new, tf), F32),
                        pltpu.VMEM((n_b, 2 * t_new, tf), F32)],
        compiler_params=_params(("arbitrary",)),
        name="sample_ffn",
    )(hn, hist, hist, w_up, w_up, w_conv, w_conv, b_conv, b_conv, w_down, h, g_post)


def kernel(x_prompt, x_sample, cache_k, cache_v, state_conv, page_table, g_pre_mix, w_in, g_v_a,
           w_s, b_s, lam_q1, lam_k1, lam_q2, lam_k2, g_subln, w_out, g_post_mix, g_pre_ffn, w_up,
           w_conv, b_conv, w_down, g_post_ffn):
    n_bp, seq, d = x_prompt.shape
    n_bs, t_new, _ = x_sample.shape
    depth = w_in.shape[0]
    assert depth == 1

    b_t = b_s[0].T
    lams = (lam_q1, lam_k1, lam_q2, lam_k2)
    ms = n_bs * t_new
    xp = x_prompt.reshape(n_bp * seq, d)
    xs = x_sample.reshape(ms, d)

    a_s, vas, qs, ks, vs, _, _, w_in_b = _in_proj(
        xs, g_pre_mix, w_in[0], g_v_a[0], w_s[0], b_t, ms, t_new, F32, True, True)
    a_out, q, k, v, kb, vb = _in_proj(
        xp, g_pre_mix, w_in_b, g_v_a[0], w_s[0], b_t, 512, CHUNK, BF16, False, False)

    b_sm = _sample_attn(qs, ks, vs, cache_k, cache_v, page_table, lams, g_subln, t_new, 8)
    sh3 = (n_bp, seq, GROUP_W)
    b_out = _prompt_attn(q.reshape(sh3), kb.reshape(sh3), vb.reshape(sh3), lams, g_subln, 512, 4)

    hs, hns, w_out_b = _out_proj(a_s, b_sm, w_out[0], xs, g_post_mix, g_pre_ffn, ms, True)
    h, hn = _out_proj(a_out, b_out.reshape(n_bp * seq, GROUP_W), w_out_b, xp,
                      g_post_mix, g_pre_ffn, 512, False)

    ys, newg, newv, w_gate_b, w_val_b, w_down_b = _sample_ffn(
        hns, hs, state_conv[0], w_up[0], w_conv[0], b_conv, w_down[0], g_post_ffn, 512, n_bs, t_new)
    conv_s = jnp.concatenate([newg, newv], axis=-1)
    tm_ffn = 512
    yp, lastg, lastv = _prompt_ffn(hn, h, w_gate_b, w_val_b, w_conv[0], b_conv, w_down_b,
                                   g_post_ffn, tm_ffn, 512, seq)
    tps = seq // tm_ffn
    conv_p = jnp.concatenate([lastg[tps - 1::tps], lastv[tps - 1::tps]], axis=-1)

    hsh = (depth, n_bp, seq, N_HEADS, HEAD_W)
    ssh = (depth, n_bs, t_new, N_HEADS, HEAD_W)
    return (yp.reshape(n_bp, seq, d), ys.reshape(n_bs, t_new, d),
            k.reshape(hsh), v.reshape(hsh), conv_p[None],
            ks.reshape(ssh), vs.reshape(ssh), conv_s[None], vas.reshape(ssh))
```

```python
import functools
import math

import jax
import jax.numpy as jnp
from jax import lax
from jax.experimental import pallas as pl
from jax.experimental.pallas import tpu as pltpu

F32 = jnp.float32
BF16 = jnp.bfloat16

EPS = 1e-6
HEAD_W = 128
HALF_W = HEAD_W // 2
N_HEADS = 8
GROUP_W = N_HEADS * HEAD_W
CHUNK = 128
PAGE = 128
CONV_W = 3
LAM_INIT = 0.8 - 0.6 * math.exp(0.0)
Q_SCALE = HALF_W ** -0.5 * math.log2(math.e)
VMEM_LIMIT = 56 * 1024 * 1024


def _params(sem):
    return pltpu.CompilerParams(dimension_semantics=sem, vmem_limit_bytes=VMEM_LIMIT)


def _rms(x, g):
    return x * lax.rsqrt(jnp.mean(x * x, axis=-1, keepdims=True) + EPS) * g


def _lam(lq1, lk1, lq2, lk2):
    a = jnp.sum(lq1[...] * lk1[...], axis=-1, keepdims=True)
    b = jnp.sum(lq2[...] * lk2[...], axis=-1, keepdims=True)
    return jnp.exp(a) - jnp.exp(b) + LAM_INIT


MXU_N = 512


def _tril_weights(ws_ref, h):
    r = lax.broadcasted_iota(jnp.int32, (CHUNK, CHUNK), 0)
    c = lax.broadcasted_iota(jnp.int32, (CHUNK, CHUNK), 1)
    return jnp.where(c <= r, ws_ref[h], 0.0).astype(BF16)


def _in_proj_kernel(x_ref, g_ref, w_ref, gva_ref, ws_ref, bt_ref, *rest, rows, emit_va, cast_w):
    rest = list(rest)
    a_ref = rest.pop(0)
    va_ref = rest.pop(0) if emit_va else None
    q_ref, k_ref, v_ref, kb_ref, vb_ref = rest[:5]
    rest = rest[5:]
    wb_ref = rest.pop(0) if cast_w else w_ref
    xg_ref, r_ref, u_ref = rest
    j = pl.program_id(1)
    tm = xg_ref.shape[0]
    hpc = MXU_N // HEAD_W

    if cast_w:
        wb_ref[...] = w_ref[...].astype(BF16)

    def project(c):
        z = jnp.dot(xg_ref[...], wb_ref[:, c * MXU_N:(c + 1) * MXU_N], preferred_element_type=F32)
        return z * r_ref[...]

    def chunks():
        return [(c, slice(c * MXU_N, (c + 1) * MXU_N)) for c in range(GROUP_W // MXU_N)]

    @pl.when(j == 0)
    def _():
        x = x_ref[...]
        xg_ref[...] = (x * g_ref[...]).astype(BF16)
        r_ref[...] = lax.rsqrt(jnp.mean(x * x, axis=-1, keepdims=True) + EPS)
        for c, cs in chunks():
            u_ref[:, cs] = jax.nn.gelu(project(c), approximate=True).astype(u_ref.dtype)

    @pl.when(j == 1)
    def _():
        for c, _ in chunks():
            a = jax.nn.gelu(project(c), approximate=True)
            for hh in range(hpc):
                h = c * hpc + hh
                sl = slice(h * HEAD_W, (h + 1) * HEAD_W)
                va = _rms(a[:, hh * HEAD_W:(hh + 1) * HEAD_W], gva_ref[h:h + 1, :])
                if emit_va:
                    va_ref[:, sl] = va
                w = _tril_weights(ws_ref, h)
                bias = bt_ref[:, h:h + 1]
                for t in range(tm // rows):
                    rs = slice(t * rows, (t + 1) * rows)
                    vc = va[rs]
                    if rows < CHUNK:
                        vc = jnp.concatenate([vc, jnp.zeros((CHUNK - rows, HEAD_W), F32)], axis=0)
                    mixed = jnp.dot(w, vc.astype(BF16), preferred_element_type=F32) + bias
                    a_ref[rs, sl] = (u_ref[rs, sl].astype(F32) * mixed[:rows]).astype(a_ref.dtype)

    @pl.when(j == 2)
    def _():
        for c, cs in chunks():
            q_ref[:, cs] = (project(c) * Q_SCALE).astype(q_ref.dtype)

    @pl.when(j == 3)
    def _():
        for c, cs in chunks():
            z = project(c)
            k_ref[:, cs] = z
            kb_ref[:, cs] = z.astype(BF16)

    @pl.when(j == 4)
    def _():
        for c, cs in chunks():
            z = project(c)
            v_ref[:, cs] = z
            vb_ref[:, cs] = z.astype(BF16)


def _in_proj(x2d, g, w, gva, w_s, b_t, tm, rows, act_dtype, emit_va, cast_w):
    m, d = x2d.shape
    row = lambda i, j: (i, 0)
    const = lambda i, j: (0, 0)
    blk = pl.BlockSpec((tm, GROUP_W), row)
    wspec = pl.BlockSpec((d, GROUP_W), lambda i, j: (0, j))
    act = jax.ShapeDtypeStruct((m, GROUP_W), act_dtype)
    out_shape = ([act] + ([jax.ShapeDtypeStruct((m, GROUP_W), F32)] if emit_va else [])
                 + [act, jax.ShapeDtypeStruct((m, GROUP_W), F32),
                    jax.ShapeDtypeStruct((m, GROUP_W), F32),
                    jax.ShapeDtypeStruct((m, GROUP_W), BF16),
                    jax.ShapeDtypeStruct((m, GROUP_W), BF16)])
    out_specs = [blk] * len(out_shape)
    if cast_w:
        assert m == tm, "the bf16 weight copy is written once per column group"
        out_shape.append(jax.ShapeDtypeStruct(w.shape, BF16))
        out_specs.append(wspec)
    return pl.pallas_call(
        functools.partial(_in_proj_kernel, rows=rows, emit_va=emit_va, cast_w=cast_w),
        grid=(m // tm, 5),
        in_specs=[pl.BlockSpec((tm, d), row), pl.BlockSpec((1, d), const), wspec,
                  pl.BlockSpec((N_HEADS, HEAD_W), const),
                  pl.BlockSpec((N_HEADS, CHUNK, CHUNK), lambda i, j: (0, 0, 0)),
                  pl.BlockSpec((CHUNK, N_HEADS), const)],
        out_specs=out_specs,
        out_shape=out_shape,
        scratch_shapes=[pltpu.VMEM((tm, d), BF16), pltpu.VMEM((tm, 1), F32),
                        pltpu.VMEM((tm, GROUP_W), act_dtype)],
        compiler_params=_params(("arbitrary", "arbitrary")),
        name="in_proj",
    )(x2d, g, w, gva, w_s, b_t)


ONES_PAD = 16


def _subln(o, lam, g):
    t = o.shape[0] // 2
    d = o[:t] - lam * o[t:]
    return _rms(d, g) * (1.0 - LAM_INIT)


def _prompt_attn_kernel(lq1, lk1, lq2, lk2, g_ref, q_ref, k_ref, v_ref, o_ref,
                        qt_ref, vt_ref, m_ref, acc_ref, s_ref, *, tq, n_chain):
    qi = pl.program_id(2)

    @pl.when(qi == 0)
    def _():
        ones_row = lax.broadcasted_iota(jnp.int32, (ONES_PAD, tq), 0) == 0
        for c in range(vt_ref.shape[0]):
            vt_ref[c, :HEAD_W, :] = v_ref[0, c * tq:(c + 1) * tq, :].astype(F32).T.astype(BF16)
            vt_ref[c, HEAD_W:, :] = jnp.where(ones_row, 1.0, 0.0).astype(BF16)

    qf = q_ref[0].astype(F32).T
    dim = lax.broadcasted_iota(jnp.int32, qf.shape, 0)
    qt_ref[:, :tq] = jnp.where(dim < HALF_W, qf, 0.0).astype(BF16)
    qt_ref[:, tq:] = jnp.where(dim >= HALF_W, qf, 0.0).astype(BF16)
    m_ref[...] = jnp.full(m_ref.shape, -jnp.inf, F32)
    acc_ref[...] = jnp.zeros(acc_ref.shape, F32)
    cw = 2 * tq // n_chain

    def scores(kb, slot, masked):
        k = k_ref[0, pl.ds(pl.multiple_of(kb * tq, tq), tq), :]
        for c in range(n_chain):
            cs = slice(c * cw, (c + 1) * cw)
            s = jnp.dot(k, qt_ref[:, cs], preferred_element_type=F32)
            if masked:
                key = lax.broadcasted_iota(jnp.int32, s.shape, 0)
                qry = lax.broadcasted_iota(jnp.int32, s.shape, 1) + (c * cw) % tq
                s = jnp.where(key <= qry, s, -jnp.inf)
            s_ref[slot, c] = s

    def absorb(kb, slot):
        vt = vt_ref[kb]
        for c in range(n_chain):
            cs = slice(c * cw, (c + 1) * cw)
            s = s_ref[slot, c]
            m_old = m_ref[:, cs]
            m_new = jnp.maximum(m_old, jnp.max(s, axis=0, keepdims=True))
            alpha = jnp.exp2(m_old - m_new)
            p = jnp.exp2(s - m_new)
            acc_ref[:, cs] = alpha * acc_ref[:, cs] + jnp.dot(vt, p.astype(BF16),
                                                              preferred_element_type=F32)
            m_ref[:, cs] = m_new

    @pl.when(qi == 0)
    def _():
        scores(0, 0, True)

    @pl.when(qi > 0)
    def _():
        scores(0, 0, False)

    n_pair = lax.shift_right_logical(jnp.maximum(qi - 1, 0), 1)

    def pair(t, carry):
        kb = 2 * t
        scores(kb + 1, 1, False)
        absorb(kb, 0)
        scores(kb + 2, 0, False)
        absorb(kb + 1, 1)
        return carry

    lax.fori_loop(0, n_pair, pair, 0)
    left = qi - 2 * n_pair

    @pl.when(left == 0)
    def _():
        absorb(qi, 0)

    @pl.when(left == 1)
    def _():
        scores(qi, 1, True)
        absorb(qi - 1, 0)
        absorb(qi, 1)

    @pl.when(left == 2)
    def _():
        scores(qi - 1, 1, False)
        absorb(qi - 2, 0)
        scores(qi, 0, True)
        absorb(qi - 1, 1)
        absorb(qi, 0)

    lam = _lam(lq1, lk1, lq2, lk2)
    o = acc_ref[:HEAD_W, :] / acc_ref[HEAD_W:HEAD_W + 1, :]
    d = o[:, :tq] - lam * o[:, tq:]
    dn = d * lax.rsqrt(jnp.mean(d * d, axis=0, keepdims=True) + EPS)
    o_ref[0] = (dn.T * g_ref[...] * (1.0 - LAM_INIT)).astype(o_ref.dtype)


def _prompt_attn(q, k, v, lams, g_subln, tq, n_chain):
    b, s, _ = q.shape
    lam_spec = pl.BlockSpec((1, HALF_W), lambda bi, h, qi: (0, 0))
    qspec = pl.BlockSpec((1, tq, HEAD_W), lambda bi, h, qi: (bi, qi, h))
    kvspec = pl.BlockSpec((1, s, HEAD_W), lambda bi, h, qi: (bi, 0, h))
    return pl.pallas_call(
        functools.partial(_prompt_attn_kernel, tq=tq, n_chain=n_chain),
        grid=(b, N_HEADS, s // tq),
        in_specs=[lam_spec] * 4 + [pl.BlockSpec((1, HEAD_W), lambda bi, h, qi: (0, 0)),
                                   qspec, kvspec, kvspec],
        out_specs=qspec,
        out_shape=jax.ShapeDtypeStruct((b, s, GROUP_W), BF16),
        scratch_shapes=[pltpu.VMEM((HEAD_W, 2 * tq), BF16),
                        pltpu.VMEM((s // tq, HEAD_W + ONES_PAD, tq), BF16),
                        pltpu.VMEM((1, 2 * tq), F32),
                        pltpu.VMEM((HEAD_W + ONES_PAD, 2 * tq), F32),
                        pltpu.VMEM((2, n_chain, tq, 2 * tq // n_chain), F32)],
        compiler_params=_params(("arbitrary", "arbitrary", "arbitrary")),
        name="prompt_attn",
    )(*lams, g_subln, q, k, v)


def _sample_attn_kernel(pt_ref, lq1, lk1, lq2, lk2, g_ref, q_ref, kn_ref, vn_ref, *rest,
                        pages, t_new):
    k_refs = rest[:pages]
    v_refs = rest[pages:2 * pages]
    o_ref = rest[2 * pages]
    qbd_ref, m_ref, l_ref, acc_ref = rest[2 * pages + 1:]
    c = pl.program_id(1)
    nc = pl.num_programs(1)
    n_rows = qbd_ref.shape[0]

    @pl.when(c == 0)
    def _():
        q = q_ref[...].astype(F32)
        qt = jnp.concatenate([q] * (n_rows // t_new), axis=0)
        r = lax.broadcasted_iota(jnp.int32, qt.shape, 0)
        ln = lax.broadcasted_iota(jnp.int32, qt.shape, 1)
        qbd_ref[...] = jnp.where((r // t_new) == (ln // HALF_W), qt, 0.0).astype(BF16)
        m_ref[...] = jnp.full(m_ref.shape, -jnp.inf, F32)
        l_ref[...] = jnp.zeros(l_ref.shape, F32)
        acc_ref[...] = jnp.zeros(acc_ref.shape, F32)

    qbd = qbd_ref[...]

    def online(s_blocks, v_blocks):
        s = jnp.concatenate(s_blocks, axis=1) if len(s_blocks) > 1 else s_blocks[0]
        m_old = m_ref[...]
        m_new = jnp.maximum(m_old, jnp.max(s, axis=-1, keepdims=True))
        alpha = jnp.exp2(m_old - m_new)
        p = jnp.exp2(s - m_new)
        l_ref[...] = alpha * l_ref[...] + jnp.sum(p, axis=-1, keepdims=True)
        pv = None
        off = 0
        for vb in v_blocks:
            n = vb.shape[0]
            d = jnp.dot(p[:, off:off + n].astype(BF16), vb, preferred_element_type=F32)
            pv = d if pv is None else pv + d
            off += n
        acc_ref[...] = alpha * acc_ref[...] + pv
        m_ref[...] = m_new

    def scores(kb):
        return lax.dot_general(qbd, kb, (((1,), (1,)), ((), ())), preferred_element_type=F32)

    def load_page(ref):
        return jnp.concatenate([ref[pl.ds(h, PAGE, stride=N_HEADS), :] for h in range(N_HEADS)],
                               axis=1).astype(BF16)

    online([scores(load_page(r)) for r in k_refs], [load_page(r) for r in v_refs])

    @pl.when(c == nc - 1)
    def _():
        pad = jnp.zeros((PAGE - t_new, GROUP_W), F32)
        kn = jnp.concatenate([kn_ref[...], pad], axis=0).astype(BF16)
        vn = jnp.concatenate([vn_ref[...], pad], axis=0).astype(BF16)
        s = scores(kn)
        r = lax.broadcasted_iota(jnp.int32, s.shape, 0)
        col = lax.broadcasted_iota(jnp.int32, s.shape, 1)
        s = jnp.where(col <= (r % t_new), s, -jnp.inf)
        online([s], [vn])

        lam = _lam(lq1, lk1, lq2, lk2)
        o = acc_ref[...] / l_ref[...]
        for h in range(N_HEADS):
            blk = o[2 * t_new * h:2 * t_new * (h + 1), h * HEAD_W:(h + 1) * HEAD_W]
            o_ref[:, h * HEAD_W:(h + 1) * HEAD_W] = _subln(blk, lam, g_ref[...])


def _sample_attn(q, k_new, v_new, cache_k, cache_v, page_table, lams, g_subln, t_new, pages):
    n_b, n_pages = page_table.shape
    ck = cache_k.reshape(-1, HEAD_W)
    cv = cache_v.reshape(-1, HEAD_W)
    pt = page_table.reshape(-1)
    n_rows = 2 * N_HEADS * t_new
    const = lambda shape: pl.BlockSpec(shape, lambda b, c, pt_ref: (0, 0))
    rows = pl.BlockSpec((t_new, GROUP_W), lambda b, c, pt_ref: (b, 0))

    def page_spec(j):
        return pl.BlockSpec(
            (PAGE * N_HEADS, HEAD_W),
            lambda b, c, pt_ref: (pt_ref[b * n_pages + c * pages + j], 0))

    grid_spec = pltpu.PrefetchScalarGridSpec(
        num_scalar_prefetch=1,
        grid=(n_b, n_pages // pages),
        in_specs=[const((1, HALF_W))] * 4 + [const((1, HEAD_W)), rows, rows, rows]
                 + [page_spec(j) for j in range(pages)] * 2,
        out_specs=rows,
        scratch_shapes=[pltpu.VMEM((n_rows, GROUP_W), BF16),
                        pltpu.VMEM((n_rows, 1), F32),
                        pltpu.VMEM((n_rows, 1), F32),
                        pltpu.VMEM((n_rows, GROUP_W), F32)])
    return pl.pallas_call(
        functools.partial(_sample_attn_kernel, pages=pages, t_new=t_new),
        grid_spec=grid_spec,
        out_shape=jax.ShapeDtypeStruct((n_b * t_new, GROUP_W), F32),
        compiler_params=_params(("arbitrary", "arbitrary")),
        name="sample_attn",
    )(pt, *lams, g_subln, q, k_new, v_new, *([ck] * pages), *([cv] * pages))


OUT_PROJ_SUB = 128


def _out_proj_kernel(a_ref, b_ref, w_ref, x_ref, gpost_ref, gpre_ref, h_ref, hn_ref, *wb_refs):
    if wb_refs:
        wb_ref, = wb_refs
        wb_ref[...] = w_ref[...].astype(BF16)
    else:
        wb_ref = w_ref
    tm = x_ref.shape[0]
    sub = min(tm, OUT_PROJ_SUB)
    for r in range(tm // sub):
        rs = slice(r * sub, (r + 1) * sub)
        mix = jnp.dot(a_ref[rs, :].astype(BF16), wb_ref[:GROUP_W, :], preferred_element_type=F32)
        mix += jnp.dot(b_ref[rs, :].astype(BF16), wb_ref[GROUP_W:, :],
                       preferred_element_type=F32)
        h = x_ref[rs, :] + _rms(mix, gpost_ref[...])
        h_ref[rs, :] = h
        hn_ref[rs, :] = _rms(h, gpre_ref[...]).astype(hn_ref.dtype)


def _out_proj(a, b, w, x2d, g_post, g_pre, tm, cast_w):
    m, d = x2d.shape
    row = lambda i: (i, 0)
    const = lambda i: (0, 0)
    out_specs = [pl.BlockSpec((tm, d), row), pl.BlockSpec((tm, d), row)]
    out_shape = [jax.ShapeDtypeStruct((m, d), F32), jax.ShapeDtypeStruct((m, d), BF16)]
    if cast_w:
        assert m == tm, "the bf16 weight copy is written once"
        out_specs.append(pl.BlockSpec(w.shape, const))
        out_shape.append(jax.ShapeDtypeStruct(w.shape, BF16))
    return pl.pallas_call(
        _out_proj_kernel,
        grid=(m // tm,),
        in_specs=[pl.BlockSpec((tm, GROUP_W), row), pl.BlockSpec((tm, GROUP_W), row),
                  pl.BlockSpec(w.shape, const), pl.BlockSpec((tm, d), row),
                  pl.BlockSpec((1, d), const), pl.BlockSpec((1, d), const)],
        out_specs=out_specs,
        out_shape=out_shape,
        compiler_params=_params(("arbitrary",)),
        name="out_proj",
    )(a, b, w, x2d, g_post, g_pre)


HALO = 16

def _gated(cg, cv):
    return (jax.nn.gelu(cg, approximate=True) * cv).astype(BF16)


def _ffn_finish(acc, h_ref, g_ref, o_ref):
    o_ref[...] = h_ref[...] + _rms(acc, g_ref[...])


def _prompt_ffn_kernel(hn_ref, halo_ref, wg_ref, wv_ref, cwg_ref, cwv_ref, cbg_ref, cbv_ref,
                       wd_ref, h_ref, g_ref, o_ref, lastg_ref, lastv_ref,
                       xh_ref, ug_ref, uv_ref, acc_ref, *, tm, tiles_per_seq):
    i = pl.program_id(0)
    f = pl.program_id(1)

    @pl.when(f == 0)
    def _():
        starts = (i % tiles_per_seq) == 0
        xh_ref[:HALO, :] = jnp.where(starts, jnp.zeros_like(halo_ref[...]), halo_ref[...])
        xh_ref[HALO:, :] = hn_ref[...]
        acc_ref[...] = jnp.zeros(acc_ref.shape, F32)

    xh = xh_ref[...]
    ug_ref[...] = jnp.dot(xh, wg_ref[...], preferred_element_type=F32)
    uv_ref[...] = jnp.dot(xh, wv_ref[...], preferred_element_type=F32)

    def conv(u_ref, cw_ref, cb_ref):
        out = cb_ref[...]
        for j in range(CONV_W):
            lo = HALO - (CONV_W - 1) + j
            out = out + u_ref[lo:lo + tm, :] * cw_ref[j:j + 1, :]
        return out

    act = _gated(conv(ug_ref, cwg_ref, cbg_ref), conv(uv_ref, cwv_ref, cbv_ref))
    acc_ref[...] += jnp.dot(act, wd_ref[...], preferred_element_type=F32)
    lastg_ref[0] = ug_ref[HALO + tm - (CONV_W - 1):, :]
    lastv_ref[0] = uv_ref[HALO + tm - (CONV_W - 1):, :]

    @pl.when(f == pl.num_programs(1) - 1)
    def _():
        _ffn_finish(acc_ref[...], h_ref, g_ref, o_ref)


def _prompt_ffn(hn, h, w_gate, w_val, w_conv, b_conv, w_down, g_post, tm, tf, seq):
    m, d = h.shape
    d_ff = w_down.shape[0]
    nf = d_ff // tf
    nt = m // tm
    row = lambda i, f: (i, 0)
    const = lambda i, f: (0, 0)
    gate = lambda i, f: (0, f)
    val = lambda i, f: (0, nf + f)
    last = pl.BlockSpec((1, CONV_W - 1, tf), lambda i, f: (i, 0, f))
    return pl.pallas_call(
        functools.partial(_prompt_ffn_kernel, tm=tm, tiles_per_seq=seq // tm),
        grid=(nt, nf),
        in_specs=[pl.BlockSpec((tm, d), row),
                  pl.BlockSpec((HALO, d), lambda i, f: (jnp.maximum(i * (tm // HALO) - 1, 0), 0)),
                  pl.BlockSpec((d, tf), gate), pl.BlockSpec((d, tf), gate),
                  pl.BlockSpec((CONV_W, tf), gate), pl.BlockSpec((CONV_W, tf), val),
                  pl.BlockSpec((1, tf), gate), pl.BlockSpec((1, tf), val),
                  pl.BlockSpec((tf, d), lambda i, f: (f, 0)),
                  pl.BlockSpec((tm, d), row),
                  pl.BlockSpec((1, d), const)],
        out_specs=[pl.BlockSpec((tm, d), row), last, last],
        out_shape=[jax.ShapeDtypeStruct((m, d), F32),
                   jax.ShapeDtypeStruct((nt, CONV_W - 1, d_ff), F32),
                   jax.ShapeDtypeStruct((nt, CONV_W - 1, d_ff), F32)],
        scratch_shapes=[pltpu.VMEM((tm + HALO, d), BF16),
                        pltpu.VMEM((tm + HALO, tf), F32),
                        pltpu.VMEM((tm + HALO, tf), F32),
                        pltpu.VMEM((tm, d), F32)],
        compiler_params=_params(("arbitrary", "arbitrary")),
        name="prompt_ffn",
    )(hn, hn, w_gate, w_val, w_conv, w_conv, b_conv, b_conv, w_down, h, g_post)


def _sample_ffn_kernel(hn_ref, histg_ref, histv_ref, wg_ref, wv_ref, cwg_ref, cwv_ref,
                       cbg_ref, cbv_ref, wd_ref, h_ref, g_ref, o_ref, newg_ref, newv_ref,
                       wgb_ref, wvb_ref, wdb_ref, acc_ref, hpg_ref, hpv_ref, *, n_b, t_new):
    f = pl.program_id(0)
    nh = CONV_W - 1

    @pl.when(f == 0)
    def _():
        acc_ref[...] = jnp.zeros(acc_ref.shape, F32)

    wgb_ref[...] = wg_ref[...].astype(BF16)
    wvb_ref[...] = wv_ref[...].astype(BF16)
    wdb_ref[...] = wd_ref[...].astype(BF16)
    hn = hn_ref[...]
    ug = jnp.dot(hn, wgb_ref[...], preferred_element_type=F32)
    uv = jnp.dot(hn, wvb_ref[...], preferred_element_type=F32)

    def conv(u, hist_ref, cw_ref, cb_ref, new_ref, hp_ref):
        outs = []
        for b in range(n_b):
            hp_ref[b, t_new - nh:t_new, :] = hist_ref[b]
            hp_ref[b, t_new:, :] = u[b * t_new:(b + 1) * t_new, :]
            c = cb_ref[...]
            for j in range(CONV_W):
                lo = t_new - nh + j
                c = c + hp_ref[b, lo:lo + t_new, :] * cw_ref[j:j + 1, :]
            outs.append(c)
            new_ref[b] = hp_ref[b, 2 * t_new - nh:, :]
        return jnp.concatenate(outs, axis=0)

    act = _gated(conv(ug, histg_ref, cwg_ref, cbg_ref, newg_ref, hpg_ref),
                 conv(uv, histv_ref, cwv_ref, cbv_ref, newv_ref, hpv_ref))
    acc_ref[...] += jnp.dot(act, wdb_ref[...], preferred_element_type=F32)

    @pl.when(f == pl.num_programs(0) - 1)
    def _():
        _ffn_finish(acc_ref[...], h_ref, g_ref, o_ref)


def _sample_ffn(hn, h, hist, w_up, w_conv, b_conv, w_down, g_post, tf, n_b, t_new):
    m, d = h.shape
    d_ff = w_down.shape[0]
    nf = d_ff // tf
    const = lambda f: (0, 0)
    gate = lambda f: (0, f)
    val = lambda f: (0, nf + f)
    hist_g = pl.BlockSpec((n_b, CONV_W - 1, tf), lambda f: (0, 0, f))
    hist_v = pl.BlockSpec((n_b, CONV_W - 1, tf), lambda f: (0, 0, nf + f))
    return pl.pallas_call(
        functools.partial(_sample_ffn_kernel, n_b=n_b, t_new=t_new),
        grid=(nf,),
        in_specs=[pl.BlockSpec((m, d), const), hist_g, hist_v,
                  pl.BlockSpec((d, tf), gate), pl.BlockSpec((d, tf), val),
                  pl.BlockSpec((CONV_W, tf), gate), pl.BlockSpec((CONV_W, tf), val),
                  pl.BlockSpec((1, tf), gate), pl.BlockSpec((1, tf), val),
                  pl.BlockSpec((tf, d), lambda f: (f, 0)),
                  pl.BlockSpec((m, d), const), pl.BlockSpec((1, d), const)],
        out_specs=[pl.BlockSpec((m, d), const), hist_g, hist_g,
                   pl.BlockSpec((d, tf), gate), pl.BlockSpec((d, tf), gate),
                   pl.BlockSpec((tf, d), lambda f: (f, 0))],
        out_shape=[jax.ShapeDtypeStruct((m, d), F32),
                   jax.ShapeDtypeStruct((n_b, CONV_W - 1, d_ff), F32),
                   jax.ShapeDtypeStruct((n_b, CONV_W - 1, d_ff), F32),
                   jax.ShapeDtypeStruct((d, d_ff), BF16),
                   jax.ShapeDtypeStruct((d, d_ff), BF16),
                   jax.ShapeDtypeStruct((d_ff, d), BF16)],
        scratch_shapes=[pltpu.VMEM((m, d), F32),
                        pltpu.VMEM((n_b, 2 * t_new, tf), F32),
                        pltpu.VMEM((n_b, 2 * t_new, tf), F32)],
        compiler_params=_params(("arbitrary",)),
        name="sample_ffn",
    )(hn, hist, hist, w_up, w_up, w_conv, w_conv, b_conv, b_conv, w_down, h, g_post)


def kernel(x_prompt, x_sample, cache_k, cache_v, state_conv, page_table, g_pre_mix, w_in, g_v_a,
           w_s, b_s, lam_q1, lam_k1, lam_q2, lam_k2, g_subln, w_out, g_post_mix, g_pre_ffn, w_up,
           w_conv, b_conv, w_down, g_post_ffn):
    n_bp, seq, d = x_prompt.shape
    n_bs, t_new, _ = x_sample.shape
    depth = w_in.shape[0]
    assert depth == 1

    b_t = b_s[0].T
    lams = (lam_q1, lam_k1, lam_q2, lam_k2)
    ms = n_bs * t_new
    xp = x_prompt.reshape(n_bp * seq, d)
    xs = x_sample.reshape(ms, d)

    a_s, vas, qs, ks, vs, _, _, w_in_b = _in_proj(
        xs, g_pre_mix, w_in[0], g_v_a[0], w_s[0], b_t, ms, t_new, F32, True, True)
    a_out, q, k, v, kb, vb = _in_proj(
        xp, g_pre_mix, w_in_b, g_v_a[0], w_s[0], b_t, 512, CHUNK, BF16, False, False)

    b_sm = _sample_attn(qs, ks, vs, cache_k, cache_v, page_table, lams, g_subln, t_new, 16)
    sh3 = (n_bp, seq, GROUP_W)
    b_out = _prompt_attn(q.reshape(sh3), kb.reshape(sh3), vb.reshape(sh3), lams, g_subln, 512, 4)

    hs, hns, w_out_b = _out_proj(a_s, b_sm, w_out[0], xs, g_post_mix, g_pre_ffn, ms, True)
    h, hn = _out_proj(a_out, b_out.reshape(n_bp * seq, GROUP_W), w_out_b, xp,
                      g_post_mix, g_pre_ffn, 512, False)

    ys, newg, newv, w_gate_b, w_val_b, w_down_b = _sample_ffn(
        hns, hs, state_conv[0], w_up[0], w_conv[0], b_conv, w_down[0], g_post_ffn, 512, n_bs, t_new)
    conv_s = jnp.concatenate([newg, newv], axis=-1)
    tm_ffn = 512
    yp, lastg, lastv = _prompt_ffn(hn, h, w_gate_b, w_val_b, w_conv[0], b_conv, w_down_b,
                                   g_post_ffn, tm_ffn, 512, seq)
    tps = seq // tm_ffn
    conv_p = jnp.concatenate([lastg[tps - 1::tps], lastv[tps - 1::tps]], axis=-1)

    hsh = (depth, n_bp, seq, N_HEADS, HEAD_W)
    ssh = (depth, n_bs, t_new, N_HEADS, HEAD_W)
    return (yp.reshape(n_bp, seq, d), ys.reshape(n_bs, t_new, d),
            k.reshape(hsh), v.reshape(hsh), conv_p[None],
            ks.reshape(ssh), vs.reshape(ssh), conv_s[None], vas.reshape(ssh))
```

```python
import functools
import math

import jax
import jax.numpy as jnp
from jax import lax
from jax.experimental import pallas as pl
from jax.experimental.pallas import tpu as pltpu

F32 = jnp.float32
BF16 = jnp.bfloat16

EPS = 1e-6
HEAD_W = 128
HALF_W = HEAD_W // 2
N_HEADS = 8
GROUP_W = N_HEADS * HEAD_W
CHUNK = 128
PAGE = 128
CONV_W = 3
LAM_INIT = 0.8 - 0.6 * math.exp(0.0)
Q_SCALE = HALF_W ** -0.5 * math.log2(math.e)
VMEM_LIMIT = 56 * 1024 * 1024


def _params(sem):
    return pltpu.CompilerParams(dimension_semantics=sem, vmem_limit_bytes=VMEM_LIMIT)


def _rms(x, g):
    return x * lax.rsqrt(jnp.mean(x * x, axis=-1, keepdims=True) + EPS) * g


def _lam(lq1, lk1, lq2, lk2):
    a = jnp.sum(lq1[...] * lk1[...], axis=-1, keepdims=True)
    b = jnp.sum(lq2[...] * lk2[...], axis=-1, keepdims=True)
    return jnp.exp(a) - jnp.exp(b) + LAM_INIT


MXU_N = 512


def _tril_weights(ws_ref, h):
    r = lax.broadcasted_iota(jnp.int32, (CHUNK, CHUNK), 0)
    c = lax.broadcasted_iota(jnp.int32, (CHUNK, CHUNK), 1)
    return jnp.where(c <= r, ws_ref[h], 0.0).astype(BF16)


N_GROUPS = 5


def _in_proj_kernel(x_ref, g_ref, w_ref, gva_ref, ws_ref, bt_ref, *rest, rows, emit_va, cast_w,
                    resident):
    rest = list(rest)
    a_ref = rest.pop(0)
    va_ref = rest.pop(0) if emit_va else None
    q_ref, k_ref, v_ref, kb_ref, vb_ref = rest[:5]
    rest = rest[5:]
    wb_ref = rest.pop(0) if cast_w else w_ref
    xg_ref, r_ref, u_ref = rest
    tm = xg_ref.shape[0]
    hpc = MXU_N // HEAD_W

    if cast_w:
        wb_ref[...] = w_ref[...].astype(BF16)

    def project(group, c):
        col = (group * GROUP_W if resident else 0) + c * MXU_N
        z = jnp.dot(xg_ref[...], wb_ref[:, col:col + MXU_N], preferred_element_type=F32)
        return z * r_ref[...]

    def chunks():
        return [(c, slice(c * MXU_N, (c + 1) * MXU_N)) for c in range(GROUP_W // MXU_N)]

    def group_u():
        x = x_ref[...]
        xg_ref[...] = (x * g_ref[...]).astype(BF16)
        r_ref[...] = lax.rsqrt(jnp.mean(x * x, axis=-1, keepdims=True) + EPS)
        for c, cs in chunks():
            u_ref[:, cs] = jax.nn.gelu(project(0, c), approximate=True).astype(u_ref.dtype)

    def group_va():
        for c, _ in chunks():
            a = jax.nn.gelu(project(1, c), approximate=True)
            for hh in range(hpc):
                h = c * hpc + hh
                sl = slice(h * HEAD_W, (h + 1) * HEAD_W)
                va = _rms(a[:, hh * HEAD_W:(hh + 1) * HEAD_W], gva_ref[h:h + 1, :])
                if emit_va:
                    va_ref[:, sl] = va
                w = _tril_weights(ws_ref, h)
                bias = bt_ref[:, h:h + 1]
                for t in range(tm // rows):
                    rs = slice(t * rows, (t + 1) * rows)
                    vc = va[rs]
                    if rows < CHUNK:
                        vc = jnp.concatenate([vc, jnp.zeros((CHUNK - rows, HEAD_W), F32)], axis=0)
                    mixed = jnp.dot(w, vc.astype(BF16), preferred_element_type=F32) + bias
                    a_ref[rs, sl] = (u_ref[rs, sl].astype(F32) * mixed[:rows]).astype(a_ref.dtype)

    def group_q():
        for c, cs in chunks():
            q_ref[:, cs] = (project(2, c) * Q_SCALE).astype(q_ref.dtype)

    def group_kv(group, f32_ref, bf16_ref):
        for c, cs in chunks():
            z = project(group, c)
            f32_ref[:, cs] = z
            bf16_ref[:, cs] = z.astype(BF16)

    groups = [group_u, group_va, group_q,
              functools.partial(group_kv, 3, k_ref, kb_ref),
              functools.partial(group_kv, 4, v_ref, vb_ref)]
    if resident:
        for run in groups:
            run()
    else:
        for j, run in enumerate(groups):
            pl.when(pl.program_id(1) == j)(run)


def _in_proj(x2d, g, w, gva, w_s, b_t, tm, rows, act_dtype, emit_va, cast_w, resident):
    m, d = x2d.shape
    row = lambda i, j: (i, 0)
    const = lambda i, j: (0, 0)
    blk = pl.BlockSpec((tm, GROUP_W), row)
    if resident:
        wspec = pl.BlockSpec(w.shape, const, pipeline_mode=pl.Buffered(1))
    else:
        wspec = pl.BlockSpec((d, GROUP_W), lambda i, j: (0, j))
    act = jax.ShapeDtypeStruct((m, GROUP_W), act_dtype)
    out_shape = ([act] + ([jax.ShapeDtypeStruct((m, GROUP_W), F32)] if emit_va else [])
                 + [act, jax.ShapeDtypeStruct((m, GROUP_W), F32),
                    jax.ShapeDtypeStruct((m, GROUP_W), F32),
                    jax.ShapeDtypeStruct((m, GROUP_W), BF16),
                    jax.ShapeDtypeStruct((m, GROUP_W), BF16)])
    out_specs = [blk] * len(out_shape)
    if cast_w:
        assert m == tm and not resident, "the bf16 weight copy is written once per column group"
        out_shape.append(jax.ShapeDtypeStruct(w.shape, BF16))
        out_specs.append(wspec)
    return pl.pallas_call(
        functools.partial(_in_proj_kernel, rows=rows, emit_va=emit_va, cast_w=cast_w,
                          resident=resident),
        grid=(m // tm, 1 if resident else N_GROUPS),
        in_specs=[pl.BlockSpec((tm, d), row), pl.BlockSpec((1, d), const), wspec,
                  pl.BlockSpec((N_HEADS, HEAD_W), const),
                  pl.BlockSpec((N_HEADS, CHUNK, CHUNK), lambda i, j: (0, 0, 0)),
                  pl.BlockSpec((CHUNK, N_HEADS), const)],
        out_specs=out_specs,
        out_shape=out_shape,
        scratch_shapes=[pltpu.VMEM((tm, d), BF16), pltpu.VMEM((tm, 1), F32),
                        pltpu.VMEM((tm, GROUP_W), act_dtype)],
        compiler_params=_params(("arbitrary", "arbitrary")),
        name="in_proj",
    )(x2d, g, w, gva, w_s, b_t)


ONES_PAD = 16


def _subln(o, lam, g):
    t = o.shape[0] // 2
    d = o[:t] - lam * o[t:]
    return _rms(d, g) * (1.0 - LAM_INIT)


def _prompt_attn_kernel(lq1, lk1, lq2, lk2, g_ref, q_ref, k_ref, v_ref, o_ref,
                        qt_ref, vt_ref, m_ref, acc_ref, s_ref, *, tq, n_chain):
    qi = pl.program_id(2)

    @pl.when(qi == 0)
    def _():
        ones_row = lax.broadcasted_iota(jnp.int32, (ONES_PAD, tq), 0) == 0
        for c in range(vt_ref.shape[0]):
            vt_ref[c, :HEAD_W, :] = v_ref[0, c * tq:(c + 1) * tq, :].astype(F32).T.astype(BF16)
            vt_ref[c, HEAD_W:, :] = jnp.where(ones_row, 1.0, 0.0).astype(BF16)

    qf = q_ref[0].astype(F32).T
    dim = lax.broadcasted_iota(jnp.int32, qf.shape, 0)
    qt_ref[:, :tq] = jnp.where(dim < HALF_W, qf, 0.0).astype(BF16)
    qt_ref[:, tq:] = jnp.where(dim >= HALF_W, qf, 0.0).astype(BF16)
    m_ref[...] = jnp.full(m_ref.shape, -jnp.inf, F32)
    acc_ref[...] = jnp.zeros(acc_ref.shape, F32)
    cw = 2 * tq // n_chain

    def scores(kb, slot, masked):
        k = k_ref[0, pl.ds(pl.multiple_of(kb * tq, tq), tq), :]
        for c in range(n_chain):
            cs = slice(c * cw, (c + 1) * cw)
            s = jnp.dot(k, qt_ref[:, cs], preferred_element_type=F32)
            if masked:
                key = lax.broadcasted_iota(jnp.int32, s.shape, 0)
                qry = lax.broadcasted_iota(jnp.int32, s.shape, 1) + (c * cw) % tq
                s = jnp.where(key <= qry, s, -jnp.inf)
            s_ref[slot, c] = s

    def absorb(kb, slot):
        vt = vt_ref[kb]
        for c in range(n_chain):
            cs = slice(c * cw, (c + 1) * cw)
            s = s_ref[slot, c]
            m_old = m_ref[:, cs]
            m_new = jnp.maximum(m_old, jnp.max(s, axis=0, keepdims=True))
            alpha = jnp.exp2(m_old - m_new)
            p = jnp.exp2(s - m_new)
            acc_ref[:, cs] = alpha * acc_ref[:, cs] + jnp.dot(vt, p.astype(BF16),
                                                              preferred_element_type=F32)
            m_ref[:, cs] = m_new

    @pl.when(qi == 0)
    def _():
        scores(0, 0, True)

    @pl.when(qi > 0)
    def _():
        scores(0, 0, False)

    n_pair = lax.shift_right_logical(jnp.maximum(qi - 1, 0), 1)

    def pair(t, carry):
        kb = 2 * t
        scores(kb + 1, 1, False)
        absorb(kb, 0)
        scores(kb + 2, 0, False)
        absorb(kb + 1, 1)
        return carry

    lax.fori_loop(0, n_pair, pair, 0)
    left = qi - 2 * n_pair

    @pl.when(left == 0)
    def _():
        absorb(qi, 0)

    @pl.when(left == 1)
    def _():
        scores(qi, 1, True)
        absorb(qi - 1, 0)
        absorb(qi, 1)

    @pl.when(left == 2)
    def _():
        scores(qi - 1, 1, False)
        absorb(qi - 2, 0)
        scores(qi, 0, True)
        absorb(qi - 1, 1)
        absorb(qi, 0)

    lam = _lam(lq1, lk1, lq2, lk2)
    o = acc_ref[:HEAD_W, :] / acc_ref[HEAD_W:HEAD_W + 1, :]
    d = o[:, :tq] - lam * o[:, tq:]
    dn = d * lax.rsqrt(jnp.mean(d * d, axis=0, keepdims=True) + EPS)
    o_ref[0] = (dn.T * g_ref[...] * (1.0 - LAM_INIT)).astype(o_ref.dtype)


def _prompt_attn(q, k, v, lams, g_subln, tq, n_chain):
    b, s, _ = q.shape
    lam_spec = pl.BlockSpec((1, HALF_W), lambda bi, h, qi: (0, 0))
    qspec = pl.BlockSpec((1, tq, HEAD_W), lambda bi, h, qi: (bi, qi, h))
    kvspec = pl.BlockSpec((1, s, HEAD_W), lambda bi, h, qi: (bi, 0, h))
    return pl.pallas_call(
        functools.partial(_prompt_attn_kernel, tq=tq, n_chain=n_chain),
        grid=(b, N_HEADS, s // tq),
        in_specs=[lam_spec] * 4 + [pl.BlockSpec((1, HEAD_W), lambda bi, h, qi: (0, 0)),
                                   qspec, kvspec, kvspec],
        out_specs=qspec,
        out_shape=jax.ShapeDtypeStruct((b, s, GROUP_W), BF16),
        scratch_shapes=[pltpu.VMEM((HEAD_W, 2 * tq), BF16),
                        pltpu.VMEM((s // tq, HEAD_W + ONES_PAD, tq), BF16),
                        pltpu.VMEM((1, 2 * tq), F32),
                        pltpu.VMEM((HEAD_W + ONES_PAD, 2 * tq), F32),
                        pltpu.VMEM((2, n_chain, tq, 2 * tq // n_chain), F32)],
        compiler_params=_params(("arbitrary", "arbitrary", "arbitrary")),
        name="prompt_attn",
    )(*lams, g_subln, q, k, v)


def _sample_attn_kernel(pt_ref, lq1, lk1, lq2, lk2, g_ref, q_ref, kn_ref, vn_ref, *rest,
                        pages, t_new):
    k_refs = rest[:pages]
    v_refs = rest[pages:2 * pages]
    o_ref = rest[2 * pages]
    qbd_ref, m_ref, l_ref, acc_ref = rest[2 * pages + 1:]
    c = pl.program_id(1)
    nc = pl.num_programs(1)
    n_rows = qbd_ref.shape[0]

    @pl.when(c == 0)
    def _():
        q = q_ref[...].astype(F32)
        qt = jnp.concatenate([q] * (n_rows // t_new), axis=0)
        r = lax.broadcasted_iota(jnp.int32, qt.shape, 0)
        ln = lax.broadcasted_iota(jnp.int32, qt.shape, 1)
        qbd_ref[...] = jnp.where((r // t_new) == (ln // HALF_W), qt, 0.0).astype(BF16)
        m_ref[...] = jnp.full(m_ref.shape, -jnp.inf, F32)
        l_ref[...] = jnp.zeros(l_ref.shape, F32)
        acc_ref[...] = jnp.zeros(acc_ref.shape, F32)

    qbd = qbd_ref[...]

    def online(s_blocks, v_blocks):
        s = jnp.concatenate(s_blocks, axis=1) if len(s_blocks) > 1 else s_blocks[0]
        m_old = m_ref[...]
        m_new = jnp.maximum(m_old, jnp.max(s, axis=-1, keepdims=True))
        alpha = jnp.exp2(m_old - m_new)
        p = jnp.exp2(s - m_new)
        l_ref[...] = alpha * l_ref[...] + jnp.sum(p, axis=-1, keepdims=True)
        pv = None
        off = 0
        for vb in v_blocks:
            n = vb.shape[0]
            d = jnp.dot(p[:, off:off + n].astype(BF16), vb, preferred_element_type=F32)
            pv = d if pv is None else pv + d
            off += n
        acc_ref[...] = alpha * acc_ref[...] + pv
        m_ref[...] = m_new

    def scores(kb):
        return lax.dot_general(qbd, kb, (((1,), (1,)), ((), ())), preferred_element_type=F32)

    def load_page(ref):
        return jnp.concatenate([ref[pl.ds(h, PAGE, stride=N_HEADS), :] for h in range(N_HEADS)],
                               axis=1).astype(BF16)

    online([scores(load_page(r)) for r in k_refs], [load_page(r) for r in v_refs])

    @pl.when(c == nc - 1)
    def _():
        pad = jnp.zeros((PAGE - t_new, GROUP_W), F32)
        kn = jnp.concatenate([kn_ref[...], pad], axis=0).astype(BF16)
        vn = jnp.concatenate([vn_ref[...], pad], axis=0).astype(BF16)
        s = scores(kn)
        r = lax.broadcasted_iota(jnp.int32, s.shape, 0)
        col = lax.broadcasted_iota(jnp.int32, s.shape, 1)
        s = jnp.where(col <= (r % t_new), s, -jnp.inf)
        online([s], [vn])

        lam = _lam(lq1, lk1, lq2, lk2)
        o = acc_ref[...] / l_ref[...]
        for h in range(N_HEADS):
            blk = o[2 * t_new * h:2 * t_new * (h + 1), h * HEAD_W:(h + 1) * HEAD_W]
            o_ref[:, h * HEAD_W:(h + 1) * HEAD_W] = _subln(blk, lam, g_ref[...])


def _sample_attn(q, k_new, v_new, cache_k, cache_v, page_table, lams, g_subln, t_new, pages):
    n_b, n_pages = page_table.shape
    ck = cache_k.reshape(-1, HEAD_W)
    cv = cache_v.reshape(-1, HEAD_W)
    pt = page_table.reshape(-1)
    n_rows = 2 * N_HEADS * t_new
    const = lambda shape: pl.BlockSpec(shape, lambda b, c, pt_ref: (0, 0))
    rows = pl.BlockSpec((t_new, GROUP_W), lambda b, c, pt_ref: (b, 0))

    def page_spec(j):
        return pl.BlockSpec(
            (PAGE * N_HEADS, HEAD_W),
            lambda b, c, pt_ref: (pt_ref[b * n_pages + c * pages + j], 0))

    grid_spec = pltpu.PrefetchScalarGridSpec(
        num_scalar_prefetch=1,
        grid=(n_b, n_pages // pages),
        in_specs=[const((1, HALF_W))] * 4 + [const((1, HEAD_W)), rows, rows, rows]
                 + [page_spec(j) for j in range(pages)] * 2,
        out_specs=rows,
        scratch_shapes=[pltpu.VMEM((n_rows, GROUP_W), BF16),
                        pltpu.VMEM((n_rows, 1), F32),
                        pltpu.VMEM((n_rows, 1), F32),
                        pltpu.VMEM((n_rows, GROUP_W), F32)])
    return pl.pallas_call(
        functools.partial(_sample_attn_kernel, pages=pages, t_new=t_new),
        grid_spec=grid_spec,
        out_shape=jax.ShapeDtypeStruct((n_b * t_new, GROUP_W), F32),
        compiler_params=_params(("arbitrary", "arbitrary")),
        name="sample_attn",
    )(pt, *lams, g_subln, q, k_new, v_new, *([ck] * pages), *([cv] * pages))


OUT_PROJ_SUB = 128


def _out_proj_kernel(a_ref, b_ref, w_ref, x_ref, gpost_ref, gpre_ref, h_ref, hn_ref, *wb_refs):
    if wb_refs:
        wb_ref, = wb_refs
        wb_ref[...] = w_ref[...].astype(BF16)
    else:
        wb_ref = w_ref
    tm = x_ref.shape[0]
    sub = min(tm, OUT_PROJ_SUB)
    for r in range(tm // sub):
        rs = slice(r * sub, (r + 1) * sub)
        mix = jnp.dot(a_ref[rs, :].astype(BF16), wb_ref[:GROUP_W, :], preferred_element_type=F32)
        mix += jnp.dot(b_ref[rs, :].astype(BF16), wb_ref[GROUP_W:, :],
                       preferred_element_type=F32)
        h = x_ref[rs, :] + _rms(mix, gpost_ref[...])
        h_ref[rs, :] = h
        hn_ref[rs, :] = _rms(h, gpre_ref[...]).astype(hn_ref.dtype)


def _out_proj(a, b, w, x2d, g_post, g_pre, tm, cast_w):
    m, d = x2d.shape
    row = lambda i: (i, 0)
    const = lambda i: (0, 0)
    out_specs = [pl.BlockSpec((tm, d), row), pl.BlockSpec((tm, d), row)]
    out_shape = [jax.ShapeDtypeStruct((m, d), F32), jax.ShapeDtypeStruct((m, d), BF16)]
    if cast_w:
        assert m == tm, "the bf16 weight copy is written once"
        out_specs.append(pl.BlockSpec(w.shape, const))
        out_shape.append(jax.ShapeDtypeStruct(w.shape, BF16))
    return pl.pallas_call(
        _out_proj_kernel,
        grid=(m // tm,),
        in_specs=[pl.BlockSpec((tm, GROUP_W), row), pl.BlockSpec((tm, GROUP_W), row),
                  pl.BlockSpec(w.shape, const), pl.BlockSpec((tm, d), row),
                  pl.BlockSpec((1, d), const), pl.BlockSpec((1, d), const)],
        out_specs=out_specs,
        out_shape=out_shape,
        compiler_params=_params(("arbitrary",)),
        name="out_proj",
    )(a, b, w, x2d, g_post, g_pre)


HALO = 16

def _gated(cg, cv):
    return (jax.nn.gelu(cg, approximate=True) * cv).astype(BF16)


def _ffn_finish(acc, h_ref, g_ref, o_ref):
    o_ref[...] = h_ref[...] + _rms(acc, g_ref[...])


def _prompt_ffn_kernel(hn_ref, halo_ref, wg_ref, wv_ref, cwg_ref, cwv_ref, cbg_ref, cbv_ref,
                       wd_ref, h_ref, g_ref, o_ref, lastg_ref, lastv_ref,
                       xh_ref, ug_ref, uv_ref, acc_ref, *, tm, tiles_per_seq):
    i = pl.program_id(0)
    f = pl.program_id(1)

    @pl.when(f == 0)
    def _():
        starts = (i % tiles_per_seq) == 0
        xh_ref[:HALO, :] = jnp.where(starts, jnp.zeros_like(halo_ref[...]), halo_ref[...])
        xh_ref[HALO:, :] = hn_ref[...]
        acc_ref[...] = jnp.zeros(acc_ref.shape, F32)

    xh = xh_ref[...]
    ug_ref[...] = jnp.dot(xh, wg_ref[...], preferred_element_type=F32)
    uv_ref[...] = jnp.dot(xh, wv_ref[...], preferred_element_type=F32)

    def conv(u_ref, cw_ref, cb_ref):
        out = cb_ref[...]
        for j in range(CONV_W):
            lo = HALO - (CONV_W - 1) + j
            out = out + u_ref[lo:lo + tm, :] * cw_ref[j:j + 1, :]
        return out

    act = _gated(conv(ug_ref, cwg_ref, cbg_ref), conv(uv_ref, cwv_ref, cbv_ref))
    acc_ref[...] += jnp.dot(act, wd_ref[...], preferred_element_type=F32)
    lastg_ref[0] = ug_ref[HALO + tm - (CONV_W - 1):, :]
    lastv_ref[0] = uv_ref[HALO + tm - (CONV_W - 1):, :]

    @pl.when(f == pl.num_programs(1) - 1)
    def _():
        _ffn_finish(acc_ref[...], h_ref, g_ref, o_ref)


def _prompt_ffn(hn, h, w_gate, w_val, w_conv, b_conv, w_down, g_post, tm, tf, seq):
    m, d = h.shape
    d_ff = w_down.shape[0]
    nf = d_ff // tf
    nt = m // tm
    row = lambda i, f: (i, 0)
    const = lambda i, f: (0, 0)
    gate = lambda i, f: (0, f)
    val = lambda i, f: (0, nf + f)
    last = pl.BlockSpec((1, CONV_W - 1, tf), lambda i, f: (i, 0, f))
    return pl.pallas_call(
        functools.partial(_prompt_ffn_kernel, tm=tm, tiles_per_seq=seq // tm),
        grid=(nt, nf),
        in_specs=[pl.BlockSpec((tm, d), row),
                  pl.BlockSpec((HALO, d), lambda i, f: (jnp.maximum(i * (tm // HALO) - 1, 0), 0)),
                  pl.BlockSpec((d, tf), gate), pl.BlockSpec((d, tf), gate),
                  pl.BlockSpec((CONV_W, tf), gate), pl.BlockSpec((CONV_W, tf), val),
                  pl.BlockSpec((1, tf), gate), pl.BlockSpec((1, tf), val),
                  pl.BlockSpec((tf, d), lambda i, f: (f, 0)),
                  pl.BlockSpec((tm, d), row),
                  pl.BlockSpec((1, d), const)],
        out_specs=[pl.BlockSpec((tm, d), row), last, last],
        out_shape=[jax.ShapeDtypeStruct((m, d), F32),
                   jax.ShapeDtypeStruct((nt, CONV_W - 1, d_ff), F32),
                   jax.ShapeDtypeStruct((nt, CONV_W - 1, d_ff), F32)],
        scratch_shapes=[pltpu.VMEM((tm + HALO, d), BF16),
                        pltpu.VMEM((tm + HALO, tf), F32),
                        pltpu.VMEM((tm + HALO, tf), F32),
                        pltpu.VMEM((tm, d), F32)],
        compiler_params=_params(("arbitrary", "arbitrary")),
        name="prompt_ffn",
    )(hn, hn, w_gate, w_val, w_conv, w_conv, b_conv, b_conv, w_down, h, g_post)


def _sample_ffn_kernel(hn_ref, histg_ref, histv_ref, wg_ref, wv_ref, cwg_ref, cwv_ref,
                       cbg_ref, cbv_ref, wd_ref, h_ref, g_ref, o_ref, newg_ref, newv_ref,
                       wgb_ref, wvb_ref, wdb_ref, acc_ref, hpg_ref, hpv_ref, *, n_b, t_new):
    f = pl.program_id(0)
    nh = CONV_W - 1

    @pl.when(f == 0)
    def _():
        acc_ref[...] = jnp.zeros(acc_ref.shape, F32)

    wgb_ref[...] = wg_ref[...].astype(BF16)
    wvb_ref[...] = wv_ref[...].astype(BF16)
    wdb_ref[...] = wd_ref[...].astype(BF16)
    hn = hn_ref[...]
    ug = jnp.dot(hn, wgb_ref[...], preferred_element_type=F32)
    uv = jnp.dot(hn, wvb_ref[...], preferred_element_type=F32)

    def conv(u, hist_ref, cw_ref, cb_ref, new_ref, hp_ref):
        outs = []
        for b in range(n_b):
            hp_ref[b, t_new - nh:t_new, :] = hist_ref[b]
            hp_ref[b, t_new:, :] = u[b * t_new:(b + 1) * t_new, :]
            c = cb_ref[...]
            for j in range(CONV_W):
                lo = t_new - nh + j
                c = c + hp_ref[b, lo:lo + t_new, :] * cw_ref[j:j + 1, :]
            outs.append(c)
            new_ref[b] = hp_ref[b, 2 * t_new - nh:, :]
        return jnp.concatenate(outs, axis=0)

    act = _gated(conv(ug, histg_ref, cwg_ref, cbg_ref, newg_ref, hpg_ref),
                 conv(uv, histv_ref, cwv_ref, cbv_ref, newv_ref, hpv_ref))
    acc_ref[...] += jnp.dot(act, wdb_ref[...], preferred_element_type=F32)

    @pl.when(f == pl.num_programs(0) - 1)
    def _():
        _ffn_finish(acc_ref[...], h_ref, g_ref, o_ref)


def _sample_ffn(hn, h, hist, w_up, w_conv, b_conv, w_down, g_post, tf, n_b, t_new):
    m, d = h.shape
    d_ff = w_down.shape[0]
    nf = d_ff // tf
    const = lambda f: (0, 0)
    gate = lambda f: (0, f)
    val = lambda f: (0, nf + f)
    hist_g = pl.BlockSpec((n_b, CONV_W - 1, tf), lambda f: (0, 0, f))
    hist_v = pl.BlockSpec((n_b, CONV_W - 1, tf), lambda f: (0, 0, nf + f))
    return pl.pallas_call(
        functools.partial(_sample_ffn_kernel, n_b=n_b, t_new=t_new),
        grid=(nf,),
        in_specs=[pl.BlockSpec((m, d), const), hist_g, hist_v,
                  pl.BlockSpec((d, tf), gate), pl.BlockSpec((d, tf), val),
                  pl.BlockSpec((CONV_W, tf), gate), pl.BlockSpec((CONV_W, tf), val),
                  pl.BlockSpec((1, tf), gate), pl.BlockSpec((1, tf), val),
                  pl.BlockSpec((tf, d), lambda f: (f, 0)),
                  pl.BlockSpec((m, d), const), pl.BlockSpec((1, d), const)],
        out_specs=[pl.BlockSpec((m, d), const), hist_g, hist_g,
                   pl.BlockSpec((d, tf), gate), pl.BlockSpec((d, tf), gate),
                   pl.BlockSpec((tf, d), lambda f: (f, 0))],
        out_shape=[jax.ShapeDtypeStruct((m, d), F32),
                   jax.ShapeDtypeStruct((n_b, CONV_W - 1, d_ff), F32),
                   jax.ShapeDtypeStruct((n_b, CONV_W - 1, d_ff), F32),
                   jax.ShapeDtypeStruct((d, d_ff), BF16),
                   jax.ShapeDtypeStruct((d, d_ff), BF16),
                   jax.ShapeDtypeStruct((d_ff, d), BF16)],
        scratch_shapes=[pltpu.VMEM((m, d), F32),
                        pltpu.VMEM((n_b, 2 * t_new, tf), F32),
                        pltpu.VMEM((n_b, 2 * t_new, tf), F32)],
        compiler_params=_params(("arbitrary",)),
        name="sample_ffn",
    )(hn, hist, hist, w_up, w_up, w_conv, w_conv, b_conv, b_conv, w_down, h, g_post)


def kernel(x_prompt, x_sample, cache_k, cache_v, state_conv, page_table, g_pre_mix, w_in, g_v_a,
           w_s, b_s, lam_q1, lam_k1, lam_q2, lam_k2, g_subln, w_out, g_post_mix, g_pre_ffn, w_up,
           w_conv, b_conv, w_down, g_post_ffn):
    n_bp, seq, d = x_prompt.shape
    n_bs, t_new, _ = x_sample.shape
    depth = w_in.shape[0]
    assert depth == 1

    b_t = b_s[0].T
    lams = (lam_q1, lam_k1, lam_q2, lam_k2)
    ms = n_bs * t_new
    xp = x_prompt.reshape(n_bp * seq, d)
    xs = x_sample.reshape(ms, d)

    a_s, vas, qs, ks, vs, _, _, w_in_b = _in_proj(
        xs, g_pre_mix, w_in[0], g_v_a[0], w_s[0], b_t, ms, t_new, F32, True, True, False)
    a_out, q, k, v, kb, vb = _in_proj(
        xp, g_pre_mix, w_in_b, g_v_a[0], w_s[0], b_t, 512, CHUNK, BF16, False, False, True)

    b_sm = _sample_attn(qs, ks, vs, cache_k, cache_v, page_table, lams, g_subln, t_new, 16)
    sh3 = (n_bp, seq, GROUP_W)
    b_out = _prompt_attn(q.reshape(sh3), kb.reshape(sh3), vb.reshape(sh3), lams, g_subln, 512, 4)

    hs, hns, w_out_b = _out_proj(a_s, b_sm, w_out[0], xs, g_post_mix, g_pre_ffn, ms, True)
    h, hn = _out_proj(a_out, b_out.reshape(n_bp * seq, GROUP_W), w_out_b, xp,
                      g_post_mix, g_pre_ffn, 512, False)

    ys, newg, newv, w_gate_b, w_val_b, w_down_b = _sample_ffn(
        hns, hs, state_conv[0], w_up[0], w_conv[0], b_conv, w_down[0], g_post_ffn, 512, n_bs, t_new)
    conv_s = jnp.concatenate([newg, newv], axis=-1)
    tm_ffn = 512
    yp, lastg, lastv = _prompt_ffn(hn, h, w_gate_b, w_val_b, w_conv[0], b_conv, w_down_b,
                                   g_post_ffn, tm_ffn, 512, seq)
    tps = seq // tm_ffn
    conv_p = jnp.concatenate([lastg[tps - 1::tps], lastv[tps - 1::tps]], axis=-1)

    hsh = (depth, n_bp, seq, N_HEADS, HEAD_W)
    ssh = (depth, n_bs, t_new, N_HEADS, HEAD_W)
    return (yp.reshape(n_bp, seq, d), ys.reshape(n_bs, t_new, d),
            k.reshape(hsh), v.reshape(hsh), conv_p[None],
            ks.reshape(ssh), vs.reshape(ssh), conv_s[None], vas.reshape(ssh))
```

```python
import functools
import math

import jax
import jax.numpy as jnp
from jax import lax
from jax.experimental import pallas as pl
from jax.experimental.pallas import tpu as pltpu

F32 = jnp.float32
BF16 = jnp.bfloat16

EPS = 1e-6
HEAD_W = 128
HALF_W = HEAD_W // 2
N_HEADS = 8
GROUP_W = N_HEADS * HEAD_W
CHUNK = 128
PAGE = 128
CONV_W = 3
LAM_INIT = 0.8 - 0.6 * math.exp(0.0)
Q_SCALE = HALF_W ** -0.5 * math.log2(math.e)
VMEM_LIMIT = 56 * 1024 * 1024


def _params(sem):
    return pltpu.CompilerParams(dimension_semantics=sem, vmem_limit_bytes=VMEM_LIMIT)


def _rms(x, g):
    return x * lax.rsqrt(jnp.mean(x * x, axis=-1, keepdims=True) + EPS) * g


def _lam(lq1, lk1, lq2, lk2):
    a = jnp.sum(lq1[...] * lk1[...], axis=-1, keepdims=True)
    b = jnp.sum(lq2[...] * lk2[...], axis=-1, keepdims=True)
    return jnp.exp(a) - jnp.exp(b) + LAM_INIT


MXU_N = 512


def _tril_weights(ws_ref, h):
    r = lax.broadcasted_iota(jnp.int32, (CHUNK, CHUNK), 0)
    c = lax.broadcasted_iota(jnp.int32, (CHUNK, CHUNK), 1)
    return jnp.where(c <= r, ws_ref[h], 0.0).astype(BF16)


N_GROUPS = 5


def _in_proj_kernel(x_ref, g_ref, w_ref, gva_ref, ws_ref, bt_ref, *rest, rows, emit_va, cast_w,
                    resident):
    rest = list(rest)
    a_ref = rest.pop(0)
    va_ref = rest.pop(0) if emit_va else None
    q_ref, k_ref, v_ref, kb_ref, vb_ref = rest[:5]
    rest = rest[5:]
    wb_ref = rest.pop(0) if cast_w else w_ref
    xg_ref, r_ref, u_ref = rest
    tm = xg_ref.shape[0]
    hpc = MXU_N // HEAD_W

    if cast_w:
        wb_ref[...] = w_ref[...].astype(BF16)

    def project(group, c):
        col = (group * GROUP_W if resident else 0) + c * MXU_N
        z = jnp.dot(xg_ref[...], wb_ref[:, col:col + MXU_N], preferred_element_type=F32)
        return z * r_ref[...]

    def chunks():
        return [(c, slice(c * MXU_N, (c + 1) * MXU_N)) for c in range(GROUP_W // MXU_N)]

    def group_u():
        x = x_ref[...]
        xg_ref[...] = (x * g_ref[...]).astype(BF16)
        r_ref[...] = lax.rsqrt(jnp.mean(x * x, axis=-1, keepdims=True) + EPS)
        for c, cs in chunks():
            u_ref[:, cs] = jax.nn.gelu(project(0, c), approximate=True).astype(u_ref.dtype)

    def group_va():
        for c, _ in chunks():
            a = jax.nn.gelu(project(1, c), approximate=True)
            for hh in range(hpc):
                h = c * hpc + hh
                sl = slice(h * HEAD_W, (h + 1) * HEAD_W)
                va = _rms(a[:, hh * HEAD_W:(hh + 1) * HEAD_W], gva_ref[h:h + 1, :])
                if emit_va:
                    va_ref[:, sl] = va
                w = _tril_weights(ws_ref, h)
                bias = bt_ref[:, h:h + 1]
                for t in range(tm // rows):
                    rs = slice(t * rows, (t + 1) * rows)
                    vc = va[rs]
                    if rows < CHUNK:
                        vc = jnp.concatenate([vc, jnp.zeros((CHUNK - rows, HEAD_W), F32)], axis=0)
                    mixed = jnp.dot(w, vc.astype(BF16), preferred_element_type=F32) + bias
                    a_ref[rs, sl] = (u_ref[rs, sl].astype(F32) * mixed[:rows]).astype(a_ref.dtype)

    def group_q():
        for c, cs in chunks():
            q_ref[:, cs] = (project(2, c) * Q_SCALE).astype(q_ref.dtype)

    def group_kv(group, f32_ref, bf16_ref):
        for c, cs in chunks():
            z = project(group, c)
            f32_ref[:, cs] = z
            bf16_ref[:, cs] = z.astype(BF16)

    groups = [group_u, group_va, group_q,
              functools.partial(group_kv, 3, k_ref, kb_ref),
              functools.partial(group_kv, 4, v_ref, vb_ref)]
    if resident:
        for run in groups:
            run()
    else:
        for j, run in enumerate(groups):
            pl.when(pl.program_id(1) == j)(run)


def _in_proj(x2d, g, w, gva, w_s, b_t, tm, rows, act_dtype, emit_va, cast_w, resident):
    m, d = x2d.shape
    row = lambda i, j: (i, 0)
    const = lambda i, j: (0, 0)
    blk = pl.BlockSpec((tm, GROUP_W), row)
    if resident:
        wspec = pl.BlockSpec(w.shape, const, pipeline_mode=pl.Buffered(1))
    else:
        wspec = pl.BlockSpec((d, GROUP_W), lambda i, j: (0, j))
    act = jax.ShapeDtypeStruct((m, GROUP_W), act_dtype)
    out_shape = ([act] + ([jax.ShapeDtypeStruct((m, GROUP_W), F32)] if emit_va else [])
                 + [act, jax.ShapeDtypeStruct((m, GROUP_W), F32),
                    jax.ShapeDtypeStruct((m, GROUP_W), F32),
                    jax.ShapeDtypeStruct((m, GROUP_W), BF16),
                    jax.ShapeDtypeStruct((m, GROUP_W), BF16)])
    out_specs = [blk] * len(out_shape)
    if cast_w:
        assert m == tm and not resident, "the bf16 weight copy is written once per column group"
        out_shape.append(jax.ShapeDtypeStruct(w.shape, BF16))
        out_specs.append(wspec)
    return pl.pallas_call(
        functools.partial(_in_proj_kernel, rows=rows, emit_va=emit_va, cast_w=cast_w,
                          resident=resident),
        grid=(m // tm, 1 if resident else N_GROUPS),
        in_specs=[pl.BlockSpec((tm, d), row), pl.BlockSpec((1, d), const), wspec,
                  pl.BlockSpec((N_HEADS, HEAD_W), const),
                  pl.BlockSpec((N_HEADS, CHUNK, CHUNK), lambda i, j: (0, 0, 0)),
                  pl.BlockSpec((CHUNK, N_HEADS), const)],
        out_specs=out_specs,
        out_shape=out_shape,
        scratch_shapes=[pltpu.VMEM((tm, d), BF16), pltpu.VMEM((tm, 1), F32),
                        pltpu.VMEM((tm, GROUP_W), act_dtype)],
        compiler_params=_params(("arbitrary", "arbitrary")),
        name="in_proj",
    )(x2d, g, w, gva, w_s, b_t)


ONES_PAD = 16


def _subln(o, lam, g):
    t = o.shape[0] // 2
    d = o[:t] - lam * o[t:]
    return _rms(d, g) * (1.0 - LAM_INIT)


def _prompt_attn_kernel(lq1, lk1, lq2, lk2, g_ref, q_ref, k_ref, v_ref, o_ref,
                        qt_ref, vt_ref, m_ref, acc_ref, s_ref, *, tq, n_chain, n_head):
    qi = pl.program_id(2)

    def lanes(hh):
        return slice(hh * HEAD_W, (hh + 1) * HEAD_W)

    @pl.when(qi == 0)
    def _():
        ones_row = lax.broadcasted_iota(jnp.int32, (ONES_PAD, tq), 0) == 0
        for hh in range(n_head):
            for c in range(vt_ref.shape[1]):
                vt_ref[hh, c, :HEAD_W, :] = v_ref[0, c * tq:(c + 1) * tq, lanes(hh)].astype(
                    F32).T.astype(BF16)
                vt_ref[hh, c, HEAD_W:, :] = jnp.where(ones_row, 1.0, 0.0).astype(BF16)

    for hh in range(n_head):
        qf = q_ref[0, :, lanes(hh)].astype(F32).T
        dim = lax.broadcasted_iota(jnp.int32, qf.shape, 0)
        qt_ref[hh, :, :tq] = jnp.where(dim < HALF_W, qf, 0.0).astype(BF16)
        qt_ref[hh, :, tq:] = jnp.where(dim >= HALF_W, qf, 0.0).astype(BF16)
    m_ref[...] = jnp.full(m_ref.shape, -jnp.inf, F32)
    acc_ref[...] = jnp.zeros(acc_ref.shape, F32)
    cw = 2 * tq // n_chain

    def scores(kb, slot, masked):
        start = pl.multiple_of(kb * tq, tq)
        for hh in range(n_head):
            k = k_ref[0, pl.ds(start, tq), lanes(hh)]
            for c in range(n_chain):
                cs = slice(c * cw, (c + 1) * cw)
                s = jnp.dot(k, qt_ref[hh, :, cs], preferred_element_type=F32)
                if masked:
                    key = lax.broadcasted_iota(jnp.int32, s.shape, 0)
                    qry = lax.broadcasted_iota(jnp.int32, s.shape, 1) + (c * cw) % tq
                    s = jnp.where(key <= qry, s, -jnp.inf)
                s_ref[hh, slot, c] = s

    def absorb(kb, slot):
        for hh in range(n_head):
            vt = vt_ref[hh, kb]
            for c in range(n_chain):
                cs = slice(c * cw, (c + 1) * cw)
                s = s_ref[hh, slot, c]
                m_old = m_ref[hh, :, cs]
                m_new = jnp.maximum(m_old, jnp.max(s, axis=0, keepdims=True))
                alpha = jnp.exp2(m_old - m_new)
                p = jnp.exp2(s - m_new)
                acc_ref[hh, :, cs] = alpha * acc_ref[hh, :, cs] + jnp.dot(
                    vt, p.astype(BF16), preferred_element_type=F32)
                m_ref[hh, :, cs] = m_new

    @pl.when(qi == 0)
    def _():
        scores(0, 0, True)

    @pl.when(qi > 0)
    def _():
        scores(0, 0, False)

    n_pair = lax.shift_right_logical(jnp.maximum(qi - 1, 0), 1)

    def pair(t, carry):
        kb = 2 * t
        scores(kb + 1, 1, False)
        absorb(kb, 0)
        scores(kb + 2, 0, False)
        absorb(kb + 1, 1)
        return carry

    lax.fori_loop(0, n_pair, pair, 0)
    left = qi - 2 * n_pair

    @pl.when(left == 0)
    def _():
        absorb(qi, 0)

    @pl.when(left == 1)
    def _():
        scores(qi, 1, True)
        absorb(qi - 1, 0)
        absorb(qi, 1)

    @pl.when(left == 2)
    def _():
        scores(qi - 1, 1, False)
        absorb(qi - 2, 0)
        scores(qi, 0, True)
        absorb(qi - 1, 1)
        absorb(qi, 0)

    lam = _lam(lq1, lk1, lq2, lk2)
    for hh in range(n_head):
        o = acc_ref[hh, :HEAD_W, :] / acc_ref[hh, HEAD_W:HEAD_W + 1, :]
        d = o[:, :tq] - lam * o[:, tq:]
        dn = d * lax.rsqrt(jnp.mean(d * d, axis=0, keepdims=True) + EPS)
        o_ref[0, :, lanes(hh)] = (dn.T * g_ref[...] * (1.0 - LAM_INIT)).astype(o_ref.dtype)


def _prompt_attn(q, k, v, lams, g_subln, tq, n_chain, n_head):
    b, s, _ = q.shape
    lam_spec = pl.BlockSpec((1, HALF_W), lambda bi, h, qi: (0, 0))
    qspec = pl.BlockSpec((1, tq, n_head * HEAD_W), lambda bi, h, qi: (bi, qi, h))
    kvspec = pl.BlockSpec((1, s, n_head * HEAD_W), lambda bi, h, qi: (bi, 0, h))
    return pl.pallas_call(
        functools.partial(_prompt_attn_kernel, tq=tq, n_chain=n_chain, n_head=n_head),
        grid=(b, N_HEADS // n_head, s // tq),
        in_specs=[lam_spec] * 4 + [pl.BlockSpec((1, HEAD_W), lambda bi, h, qi: (0, 0)),
                                   qspec, kvspec, kvspec],
        out_specs=qspec,
        out_shape=jax.ShapeDtypeStruct((b, s, GROUP_W), BF16),
        scratch_shapes=[pltpu.VMEM((n_head, HEAD_W, 2 * tq), BF16),
                        pltpu.VMEM((n_head, s // tq, HEAD_W + ONES_PAD, tq), BF16),
                        pltpu.VMEM((n_head, 1, 2 * tq), F32),
                        pltpu.VMEM((n_head, HEAD_W + ONES_PAD, 2 * tq), F32),
                        pltpu.VMEM((n_head, 2, n_chain, tq, 2 * tq // n_chain), F32)],
        compiler_params=_params(("arbitrary", "arbitrary", "arbitrary")),
        name="prompt_attn",
    )(*lams, g_subln, q, k, v)


def _sample_attn_kernel(pt_ref, lq1, lk1, lq2, lk2, g_ref, q_ref, kn_ref, vn_ref, *rest,
                        pages, t_new):
    k_refs = rest[:pages]
    v_refs = rest[pages:2 * pages]
    o_ref = rest[2 * pages]
    qbd_ref, m_ref, l_ref, acc_ref = rest[2 * pages + 1:]
    c = pl.program_id(1)
    nc = pl.num_programs(1)
    n_rows = qbd_ref.shape[0]

    @pl.when(c == 0)
    def _():
        q = q_ref[...].astype(F32)
        qt = jnp.concatenate([q] * (n_rows // t_new), axis=0)
        r = lax.broadcasted_iota(jnp.int32, qt.shape, 0)
        ln = lax.broadcasted_iota(jnp.int32, qt.shape, 1)
        qbd_ref[...] = jnp.where((r // t_new) == (ln // HALF_W), qt, 0.0).astype(BF16)
        m_ref[...] = jnp.full(m_ref.shape, -jnp.inf, F32)
        l_ref[...] = jnp.zeros(l_ref.shape, F32)
        acc_ref[...] = jnp.zeros(acc_ref.shape, F32)

    qbd = qbd_ref[...]

    def online(s_blocks, v_blocks):
        s = jnp.concatenate(s_blocks, axis=1) if len(s_blocks) > 1 else s_blocks[0]
        m_old = m_ref[...]
        m_new = jnp.maximum(m_old, jnp.max(s, axis=-1, keepdims=True))
        alpha = jnp.exp2(m_old - m_new)
        p = jnp.exp2(s - m_new)
        l_ref[...] = alpha * l_ref[...] + jnp.sum(p, axis=-1, keepdims=True)
        pv = None
        off = 0
        for vb in v_blocks:
            n = vb.shape[0]
            d = jnp.dot(p[:, off:off + n].astype(BF16), vb, preferred_element_type=F32)
            pv = d if pv is None else pv + d
            off += n
        acc_ref[...] = alpha * acc_ref[...] + pv
        m_ref[...] = m_new

    def scores(kb):
        return lax.dot_general(qbd, kb, (((1,), (1,)), ((), ())), preferred_element_type=F32)

    def load_page(ref):
        return jnp.concatenate([ref[pl.ds(h, PAGE, stride=N_HEADS), :] for h in range(N_HEADS)],
                               axis=1).astype(BF16)

    online([scores(load_page(r)) for r in k_refs], [load_page(r) for r in v_refs])

    @pl.when(c == nc - 1)
    def _():
        pad = jnp.zeros((PAGE - t_new, GROUP_W), F32)
        kn = jnp.concatenate([kn_ref[...], pad], axis=0).astype(BF16)
        vn = jnp.concatenate([vn_ref[...], pad], axis=0).astype(BF16)
        s = scores(kn)
        r = lax.broadcasted_iota(jnp.int32, s.shape, 0)
        col = lax.broadcasted_iota(jnp.int32, s.shape, 1)
        s = jnp.where(col <= (r % t_new), s, -jnp.inf)
        online([s], [vn])

        lam = _lam(lq1, lk1, lq2, lk2)
        o = acc_ref[...] / l_ref[...]
        for h in range(N_HEADS):
            blk = o[2 * t_new * h:2 * t_new * (h + 1), h * HEAD_W:(h + 1) * HEAD_W]
            o_ref[:, h * HEAD_W:(h + 1) * HEAD_W] = _subln(blk, lam, g_ref[...])


def _sample_attn(q, k_new, v_new, cache_k, cache_v, page_table, lams, g_subln, t_new, pages):
    n_b, n_pages = page_table.shape
    ck = cache_k.reshape(-1, HEAD_W)
    cv = cache_v.reshape(-1, HEAD_W)
    pt = page_table.reshape(-1)
    n_rows = 2 * N_HEADS * t_new
    const = lambda shape: pl.BlockSpec(shape, lambda b, c, pt_ref: (0, 0))
    rows = pl.BlockSpec((t_new, GROUP_W), lambda b, c, pt_ref: (b, 0))

    def page_spec(j):
        return pl.BlockSpec(
            (PAGE * N_HEADS, HEAD_W),
            lambda b, c, pt_ref: (pt_ref[b * n_pages + c * pages + j], 0))

    grid_spec = pltpu.PrefetchScalarGridSpec(
        num_scalar_prefetch=1,
        grid=(n_b, n_pages // pages),
        in_specs=[const((1, HALF_W))] * 4 + [const((1, HEAD_W)), rows, rows, rows]
                 + [page_spec(j) for j in range(pages)] * 2,
        out_specs=rows,
        scratch_shapes=[pltpu.VMEM((n_rows, GROUP_W), BF16),
                        pltpu.VMEM((n_rows, 1), F32),
                        pltpu.VMEM((n_rows, 1), F32),
                        pltpu.VMEM((n_rows, GROUP_W), F32)])
    return pl.pallas_call(
        functools.partial(_sample_attn_kernel, pages=pages, t_new=t_new),
        grid_spec=grid_spec,
        out_shape=jax.ShapeDtypeStruct((n_b * t_new, GROUP_W), F32),
        compiler_params=_params(("arbitrary", "arbitrary")),
        name="sample_attn",
    )(pt, *lams, g_subln, q, k_new, v_new, *([ck] * pages), *([cv] * pages))


OUT_PROJ_SUB = 128


def _out_proj_kernel(a_ref, b_ref, w_ref, x_ref, gpost_ref, gpre_ref, h_ref, hn_ref, *wb_refs):
    if wb_refs:
        wb_ref, = wb_refs
        wb_ref[...] = w_ref[...].astype(BF16)
    else:
        wb_ref = w_ref
    tm = x_ref.shape[0]
    sub = min(tm, OUT_PROJ_SUB)
    for r in range(tm // sub):
        rs = slice(r * sub, (r + 1) * sub)
        mix = jnp.dot(a_ref[rs, :].astype(BF16), wb_ref[:GROUP_W, :], preferred_element_type=F32)
        mix += jnp.dot(b_ref[rs, :].astype(BF16), wb_ref[GROUP_W:, :],
                       preferred_element_type=F32)
        h = x_ref[rs, :] + _rms(mix, gpost_ref[...])
        h_ref[rs, :] = h
        hn_ref[rs, :] = _rms(h, gpre_ref[...]).astype(hn_ref.dtype)


def _out_proj(a, b, w, x2d, g_post, g_pre, tm, cast_w):
    m, d = x2d.shape
    row = lambda i: (i, 0)
    const = lambda i: (0, 0)
    out_specs = [pl.BlockSpec((tm, d), row), pl.BlockSpec((tm, d), row)]
    out_shape = [jax.ShapeDtypeStruct((m, d), F32), jax.ShapeDtypeStruct((m, d), BF16)]
    if cast_w:
        assert m == tm, "the bf16 weight copy is written once"
        out_specs.append(pl.BlockSpec(w.shape, const))
        out_shape.append(jax.ShapeDtypeStruct(w.shape, BF16))
    return pl.pallas_call(
        _out_proj_kernel,
        grid=(m // tm,),
        in_specs=[pl.BlockSpec((tm, GROUP_W), row), pl.BlockSpec((tm, GROUP_W), row),
                  pl.BlockSpec(w.shape, const), pl.BlockSpec((tm, d), row),
                  pl.BlockSpec((1, d), const), pl.BlockSpec((1, d), const)],
        out_specs=out_specs,
        out_shape=out_shape,
        compiler_params=_params(("arbitrary",)),
        name="out_proj",
    )(a, b, w, x2d, g_post, g_pre)


HALO = 16

def _gated(cg, cv):
    return (jax.nn.gelu(cg, approximate=True) * cv).astype(BF16)


def _ffn_finish(acc, h_ref, g_ref, o_ref):
    o_ref[...] = h_ref[...] + _rms(acc, g_ref[...])


def _prompt_ffn_kernel(hn_ref, halo_ref, wg_ref, wv_ref, cwg_ref, cwv_ref, cbg_ref, cbv_ref,
                       wd_ref, h_ref, g_ref, o_ref, lastg_ref, lastv_ref,
                       xh_ref, ug_ref, uv_ref, acc_ref, *, tm, tiles_per_seq):
    i = pl.program_id(0)
    f = pl.program_id(1)

    @pl.when(f == 0)
    def _():
        starts = (i % tiles_per_seq) == 0
        xh_ref[:HALO, :] = jnp.where(starts, jnp.zeros_like(halo_ref[...]), halo_ref[...])
        xh_ref[HALO:, :] = hn_ref[...]
        acc_ref[...] = jnp.zeros(acc_ref.shape, F32)

    xh = xh_ref[...]
    ug_ref[...] = jnp.dot(xh, wg_ref[...], preferred_element_type=F32)
    uv_ref[...] = jnp.dot(xh, wv_ref[...], preferred_element_type=F32)

    def conv(u_ref, cw_ref, cb_ref):
        out = cb_ref[...]
        for j in range(CONV_W):
            lo = HALO - (CONV_W - 1) + j
            out = out + u_ref[lo:lo + tm, :] * cw_ref[j:j + 1, :]
        return out

    act = _gated(conv(ug_ref, cwg_ref, cbg_ref), conv(uv_ref, cwv_ref, cbv_ref))
    acc_ref[...] += jnp.dot(act, wd_ref[...], preferred_element_type=F32)
    lastg_ref[0] = ug_ref[HALO + tm - (CONV_W - 1):, :]
    lastv_ref[0] = uv_ref[HALO + tm - (CONV_W - 1):, :]

    @pl.when(f == pl.num_programs(1) - 1)
    def _():
        _ffn_finish(acc_ref[...], h_ref, g_ref, o_ref)


def _prompt_ffn(hn, h, w_gate, w_val, w_conv, b_conv, w_down, g_post, tm, tf, seq):
    m, d = h.shape
    d_ff = w_down.shape[0]
    nf = d_ff // tf
    nt = m // tm
    row = lambda i, f: (i, 0)
    const = lambda i, f: (0, 0)
    gate = lambda i, f: (0, f)
    val = lambda i, f: (0, nf + f)
    last = pl.BlockSpec((1, CONV_W - 1, tf), lambda i, f: (i, 0, f))
    return pl.pallas_call(
        functools.partial(_prompt_ffn_kernel, tm=tm, tiles_per_seq=seq // tm),
        grid=(nt, nf),
        in_specs=[pl.BlockSpec((tm, d), row),
                  pl.BlockSpec((HALO, d), lambda i, f: (jnp.maximum(i * (tm // HALO) - 1, 0), 0)),
                  pl.BlockSpec((d, tf), gate), pl.BlockSpec((d, tf), gate),
                  pl.BlockSpec((CONV_W, tf), gate), pl.BlockSpec((CONV_W, tf), val),
                  pl.BlockSpec((1, tf), gate), pl.BlockSpec((1, tf), val),
                  pl.BlockSpec((tf, d), lambda i, f: (f, 0)),
                  pl.BlockSpec((tm, d), row),
                  pl.BlockSpec((1, d), const)],
        out_specs=[pl.BlockSpec((tm, d), row), last, last],
        out_shape=[jax.ShapeDtypeStruct((m, d), F32),
                   jax.ShapeDtypeStruct((nt, CONV_W - 1, d_ff), F32),
                   jax.ShapeDtypeStruct((nt, CONV_W - 1, d_ff), F32)],
        scratch_shapes=[pltpu.VMEM((tm + HALO, d), BF16),
                        pltpu.VMEM((tm + HALO, tf), F32),
                        pltpu.VMEM((tm + HALO, tf), F32),
                        pltpu.VMEM((tm, d), F32)],
        compiler_params=_params(("arbitrary", "arbitrary")),
        name="prompt_ffn",
    )(hn, hn, w_gate, w_val, w_conv, w_conv, b_conv, b_conv, w_down, h, g_post)


def _sample_ffn_kernel(hn_ref, histg_ref, histv_ref, wg_ref, wv_ref, cwg_ref, cwv_ref,
                       cbg_ref, cbv_ref, wd_ref, h_ref, g_ref, o_ref, newg_ref, newv_ref,
                       wgb_ref, wvb_ref, wdb_ref, acc_ref, hpg_ref, hpv_ref, *, n_b, t_new):
    f = pl.program_id(0)
    nh = CONV_W - 1

    @pl.when(f == 0)
    def _():
        acc_ref[...] = jnp.zeros(acc_ref.shape, F32)

    wgb_ref[...] = wg_ref[...].astype(BF16)
    wvb_ref[...] = wv_ref[...].astype(BF16)
    wdb_ref[...] = wd_ref[...].astype(BF16)
    hn = hn_ref[...]
    ug = jnp.dot(hn, wgb_ref[...], preferred_element_type=F32)
    uv = jnp.dot(hn, wvb_ref[...], preferred_element_type=F32)

    def conv(u, hist_ref, cw_ref, cb_ref, new_ref, hp_ref):
        outs = []
        for b in range(n_b):
            hp_ref[b, t_new - nh:t_new, :] = hist_ref[b]
            hp_ref[b, t_new:, :] = u[b * t_new:(b + 1) * t_new, :]
            c = cb_ref[...]
            for j in range(CONV_W):
                lo = t_new - nh + j
                c = c + hp_ref[b, lo:lo + t_new, :] * cw_ref[j:j + 1, :]
            outs.append(c)
            new_ref[b] = hp_ref[b, 2 * t_new - nh:, :]
        return jnp.concatenate(outs, axis=0)

    act = _gated(conv(ug, histg_ref, cwg_ref, cbg_ref, newg_ref, hpg_ref),
                 conv(uv, histv_ref, cwv_ref, cbv_ref, newv_ref, hpv_ref))
    acc_ref[...] += jnp.dot(act, wdb_ref[...], preferred_element_type=F32)

    @pl.when(f == pl.num_programs(0) - 1)
    def _():
        _ffn_finish(acc_ref[...], h_ref, g_ref, o_ref)


def _sample_ffn(hn, h, hist, w_up, w_conv, b_conv, w_down, g_post, tf, n_b, t_new):
    m, d = h.shape
    d_ff = w_down.shape[0]
    nf = d_ff // tf
    const = lambda f: (0, 0)
    gate = lambda f: (0, f)
    val = lambda f: (0, nf + f)
    hist_g = pl.BlockSpec((n_b, CONV_W - 1, tf), lambda f: (0, 0, f))
    hist_v = pl.BlockSpec((n_b, CONV_W - 1, tf), lambda f: (0, 0, nf + f))
    return pl.pallas_call(
        functools.partial(_sample_ffn_kernel, n_b=n_b, t_new=t_new),
        grid=(nf,),
        in_specs=[pl.BlockSpec((m, d), const), hist_g, hist_v,
                  pl.BlockSpec((d, tf), gate), pl.BlockSpec((d, tf), val),
                  pl.BlockSpec((CONV_W, tf), gate), pl.BlockSpec((CONV_W, tf), val),
                  pl.BlockSpec((1, tf), gate), pl.BlockSpec((1, tf), val),
                  pl.BlockSpec((tf, d), lambda f: (f, 0)),
                  pl.BlockSpec((m, d), const), pl.BlockSpec((1, d), const)],
        out_specs=[pl.BlockSpec((m, d), const), hist_g, hist_g,
                   pl.BlockSpec((d, tf), gate), pl.BlockSpec((d, tf), gate),
                   pl.BlockSpec((tf, d), lambda f: (f, 0))],
        out_shape=[jax.ShapeDtypeStruct((m, d), F32),
                   jax.ShapeDtypeStruct((n_b, CONV_W - 1, d_ff), F32),
                   jax.ShapeDtypeStruct((n_b, CONV_W - 1, d_ff), F32),
                   jax.ShapeDtypeStruct((d, d_ff), BF16),
                   jax.ShapeDtypeStruct((d, d_ff), BF16),
                   jax.ShapeDtypeStruct((d_ff, d), BF16)],
        scratch_shapes=[pltpu.VMEM((m, d), F32),
                        pltpu.VMEM((n_b, 2 * t_new, tf), F32),
                        pltpu.VMEM((n_b, 2 * t_new, tf), F32)],
        compiler_params=_params(("arbitrary",)),
        name="sample_ffn",
    )(hn, hist, hist, w_up, w_up, w_conv, w_conv, b_conv, b_conv, w_down, h, g_post)


def kernel(x_prompt, x_sample, cache_k, cache_v, state_conv, page_table, g_pre_mix, w_in, g_v_a,
           w_s, b_s, lam_q1, lam_k1, lam_q2, lam_k2, g_subln, w_out, g_post_mix, g_pre_ffn, w_up,
           w_conv, b_conv, w_down, g_post_ffn):
    n_bp, seq, d = x_prompt.shape
    n_bs, t_new, _ = x_sample.shape
    depth = w_in.shape[0]
    assert depth == 1

    b_t = b_s[0].T
    lams = (lam_q1, lam_k1, lam_q2, lam_k2)
    ms = n_bs * t_new
    xp = x_prompt.reshape(n_bp * seq, d)
    xs = x_sample.reshape(ms, d)

    a_s, vas, qs, ks, vs, _, _, w_in_b = _in_proj(
        xs, g_pre_mix, w_in[0], g_v_a[0], w_s[0], b_t, ms, t_new, F32, True, True, False)
    a_out, q, k, v, kb, vb = _in_proj(
        xp, g_pre_mix, w_in_b, g_v_a[0], w_s[0], b_t, 512, CHUNK, BF16, False, False, True)

    b_sm = _sample_attn(qs, ks, vs, cache_k, cache_v, page_table, lams, g_subln, t_new, 16)
    sh3 = (n_bp, seq, GROUP_W)
    b_out = _prompt_attn(q.reshape(sh3), kb.reshape(sh3), vb.reshape(sh3), lams, g_subln, 512, 4, 4)

    hs, hns, w_out_b = _out_proj(a_s, b_sm, w_out[0], xs, g_post_mix, g_pre_ffn, ms, True)
    h, hn = _out_proj(a_out, b_out.reshape(n_bp * seq, GROUP_W), w_out_b, xp,
                      g_post_mix, g_pre_ffn, 512, False)

    ys, newg, newv, w_gate_b, w_val_b, w_down_b = _sample_ffn(
        hns, hs, state_conv[0], w_up[0], w_conv[0], b_conv, w_down[0], g_post_ffn, 512, n_bs, t_new)
    conv_s = jnp.concatenate([newg, newv], axis=-1)
    tm_ffn = 512
    yp, lastg, lastv = _prompt_ffn(hn, h, w_gate_b, w_val_b, w_conv[0], b_conv, w_down_b,
                                   g_post_ffn, tm_ffn, 512, seq)
    tps = seq // tm_ffn
    conv_p = jnp.concatenate([lastg[tps - 1::tps], lastv[tps - 1::tps]], axis=-1)

    hsh = (depth, n_bp, seq, N_HEADS, HEAD_W)
    ssh = (depth, n_bs, t_new, N_HEADS, HEAD_W)
    return (yp.reshape(n_bp, seq, d), ys.reshape(n_bs, t_new, d),
            k.reshape(hsh), v.reshape(hsh), conv_p[None],
            ks.reshape(ssh), vs.reshape(ssh), conv_s[None], vas.reshape(ssh))
```

```python
import functools
import math

import jax
import jax.numpy as jnp
from jax import lax
from jax.experimental import pallas as pl
from jax.experimental.pallas import tpu as pltpu

F32 = jnp.float32
BF16 = jnp.bfloat16

EPS = 1e-6
HEAD_W = 128
HALF_W = HEAD_W // 2
N_HEADS = 8
GROUP_W = N_HEADS * HEAD_W
CHUNK = 128
PAGE = 128
CONV_W = 3
LAM_INIT = 0.8 - 0.6 * math.exp(0.0)
Q_SCALE = HALF_W ** -0.5 * math.log2(math.e)
VMEM_LIMIT = 56 * 1024 * 1024


def _params(sem):
    return pltpu.CompilerParams(dimension_semantics=sem, vmem_limit_bytes=VMEM_LIMIT)


def _rms(x, g):
    return x * lax.rsqrt(jnp.mean(x * x, axis=-1, keepdims=True) + EPS) * g


def _lam(lq1, lk1, lq2, lk2):
    a = jnp.sum(lq1[...] * lk1[...], axis=-1, keepdims=True)
    b = jnp.sum(lq2[...] * lk2[...], axis=-1, keepdims=True)
    return jnp.exp(a) - jnp.exp(b) + LAM_INIT


MXU_N = 512


def _tril_weights(ws_ref, h):
    r = lax.broadcasted_iota(jnp.int32, (CHUNK, CHUNK), 0)
    c = lax.broadcasted_iota(jnp.int32, (CHUNK, CHUNK), 1)
    return jnp.where(c <= r, ws_ref[h], 0.0).astype(BF16)


N_GROUPS = 5


def _in_proj_kernel(x_ref, g_ref, w_ref, gva_ref, ws_ref, bt_ref, *rest, rows, emit_va, cast_w,
                    resident):
    rest = list(rest)
    a_ref = rest.pop(0)
    va_ref = rest.pop(0) if emit_va else None
    q_ref, k_ref, v_ref, kb_ref, vb_ref = rest[:5]
    rest = rest[5:]
    wb_ref = rest.pop(0) if cast_w else w_ref
    xg_ref, r_ref, u_ref = rest
    tm = xg_ref.shape[0]
    hpc = MXU_N // HEAD_W

    if cast_w:
        wb_ref[...] = w_ref[...].astype(BF16)

    def project(group, c):
        col = (group * GROUP_W if resident else 0) + c * MXU_N
        z = jnp.dot(xg_ref[...], wb_ref[:, col:col + MXU_N], preferred_element_type=F32)
        return z * r_ref[...]

    def chunks():
        return [(c, slice(c * MXU_N, (c + 1) * MXU_N)) for c in range(GROUP_W // MXU_N)]

    def group_u():
        x = x_ref[...]
        xg_ref[...] = (x * g_ref[...]).astype(BF16)
        r_ref[...] = lax.rsqrt(jnp.mean(x * x, axis=-1, keepdims=True) + EPS)
        for c, cs in chunks():
            u_ref[:, cs] = jax.nn.gelu(project(0, c), approximate=True).astype(u_ref.dtype)

    def group_va():
        for c, _ in chunks():
            a = jax.nn.gelu(project(1, c), approximate=True)
            for hh in range(hpc):
                h = c * hpc + hh
                sl = slice(h * HEAD_W, (h + 1) * HEAD_W)
                va = _rms(a[:, hh * HEAD_W:(hh + 1) * HEAD_W], gva_ref[h:h + 1, :])
                if emit_va:
                    va_ref[:, sl] = va
                w = _tril_weights(ws_ref, h)
                bias = bt_ref[:, h:h + 1]
                for t in range(tm // rows):
                    rs = slice(t * rows, (t + 1) * rows)
                    vc = va[rs]
                    if rows < CHUNK:
                        vc = jnp.concatenate([vc, jnp.zeros((CHUNK - rows, HEAD_W), F32)], axis=0)
                    mixed = jnp.dot(w, vc.astype(BF16), preferred_element_type=F32) + bias
                    a_ref[rs, sl] = (u_ref[rs, sl].astype(F32) * mixed[:rows]).astype(a_ref.dtype)

    def group_q():
        for c, cs in chunks():
            q_ref[:, cs] = (project(2, c) * Q_SCALE).astype(q_ref.dtype)

    def group_kv(group, f32_ref, bf16_ref):
        for c, cs in chunks():
            z = project(group, c)
            f32_ref[:, cs] = z
            bf16_ref[:, cs] = z.astype(BF16)

    groups = [group_u, group_va, group_q,
              functools.partial(group_kv, 3, k_ref, kb_ref),
              functools.partial(group_kv, 4, v_ref, vb_ref)]
    if resident:
        for run in groups:
            run()
    else:
        for j, run in enumerate(groups):
            pl.when(pl.program_id(1) == j)(run)


def _in_proj(x2d, g, w, gva, w_s, b_t, tm, rows, act_dtype, emit_va, cast_w, resident):
    m, d = x2d.shape
    row = lambda i, j: (i, 0)
    const = lambda i, j: (0, 0)
    blk = pl.BlockSpec((tm, GROUP_W), row)
    if resident:
        wspec = pl.BlockSpec(w.shape, const, pipeline_mode=pl.Buffered(1))
    else:
        wspec = pl.BlockSpec((d, GROUP_W), lambda i, j: (0, j))
    act = jax.ShapeDtypeStruct((m, GROUP_W), act_dtype)
    out_shape = ([act] + ([jax.ShapeDtypeStruct((m, GROUP_W), F32)] if emit_va else [])
                 + [act, jax.ShapeDtypeStruct((m, GROUP_W), F32),
                    jax.ShapeDtypeStruct((m, GROUP_W), F32),
                    jax.ShapeDtypeStruct((m, GROUP_W), BF16),
                    jax.ShapeDtypeStruct((m, GROUP_W), BF16)])
    out_specs = [blk] * len(out_shape)
    if cast_w:
        assert m == tm and not resident, "the bf16 weight copy is written once per column group"
        out_shape.append(jax.ShapeDtypeStruct(w.shape, BF16))
        out_specs.append(wspec)
    return pl.pallas_call(
        functools.partial(_in_proj_kernel, rows=rows, emit_va=emit_va, cast_w=cast_w,
                          resident=resident),
        grid=(m // tm, 1 if resident else N_GROUPS),
        in_specs=[pl.BlockSpec((tm, d), row), pl.BlockSpec((1, d), const), wspec,
                  pl.BlockSpec((N_HEADS, HEAD_W), const),
                  pl.BlockSpec((N_HEADS, CHUNK, CHUNK), lambda i, j: (0, 0, 0)),
                  pl.BlockSpec((CHUNK, N_HEADS), const)],
        out_specs=out_specs,
        out_shape=out_shape,
        scratch_shapes=[pltpu.VMEM((tm, d), BF16), pltpu.VMEM((tm, 1), F32),
                        pltpu.VMEM((tm, GROUP_W), act_dtype)],
        compiler_params=_params(("arbitrary", "arbitrary")),
        name="in_proj",
    )(x2d, g, w, gva, w_s, b_t)


ONES_PAD = 16


def _subln(o, lam, g):
    t = o.shape[0] // 2
    d = o[:t] - lam * o[t:]
    return _rms(d, g) * (1.0 - LAM_INIT)


def _prompt_attn_kernel(lq1, lk1, lq2, lk2, g_ref, q_ref, k_ref, v_ref, o_ref,
                        qt_ref, vt_ref, m_ref, acc_ref, s_ref, *, tq, n_chain, n_head):
    qi = pl.program_id(2)

    def lanes(hh):
        return slice(hh * HEAD_W, (hh + 1) * HEAD_W)

    @pl.when(qi == 0)
    def _():
        ones_row = lax.broadcasted_iota(jnp.int32, (ONES_PAD, tq), 0) == 0
        for hh in range(n_head):
            for c in range(vt_ref.shape[1]):
                vt_ref[hh, c, :HEAD_W, :] = v_ref[0, c * tq:(c + 1) * tq, lanes(hh)].astype(
                    F32).T.astype(BF16)
                vt_ref[hh, c, HEAD_W:, :] = jnp.where(ones_row, 1.0, 0.0).astype(BF16)

    for hh in range(n_head):
        qf = q_ref[0, :, lanes(hh)].astype(F32).T
        dim = lax.broadcasted_iota(jnp.int32, qf.shape, 0)
        qt_ref[hh, :, :tq] = jnp.where(dim < HALF_W, qf, 0.0).astype(BF16)
        qt_ref[hh, :, tq:] = jnp.where(dim >= HALF_W, qf, 0.0).astype(BF16)
    m_ref[...] = jnp.full(m_ref.shape, -jnp.inf, F32)
    acc_ref[...] = jnp.zeros(acc_ref.shape, F32)
    cw = 2 * tq // n_chain

    def scores(kb, slot, masked):
        start = pl.multiple_of(kb * tq, tq)
        for hh in range(n_head):
            k = k_ref[0, pl.ds(start, tq), lanes(hh)]
            for c in range(n_chain):
                cs = slice(c * cw, (c + 1) * cw)
                s = jnp.dot(k, qt_ref[hh, :, cs], preferred_element_type=F32)
                if masked:
                    key = lax.broadcasted_iota(jnp.int32, s.shape, 0)
                    qry = lax.broadcasted_iota(jnp.int32, s.shape, 1) + (c * cw) % tq
                    s = jnp.where(key <= qry, s, -jnp.inf)
                s_ref[hh, slot, c] = s

    def absorb(kb, slot):
        for hh in range(n_head):
            vt = vt_ref[hh, kb]
            for c in range(n_chain):
                cs = slice(c * cw, (c + 1) * cw)
                s = s_ref[hh, slot, c]
                m_old = m_ref[hh, :, cs]
                m_new = jnp.maximum(m_old, jnp.max(s, axis=0, keepdims=True))
                alpha = jnp.exp2(m_old - m_new)
                p = jnp.exp2(s - m_new)
                acc_ref[hh, :, cs] = alpha * acc_ref[hh, :, cs] + jnp.dot(
                    vt, p.astype(BF16), preferred_element_type=F32)
                m_ref[hh, :, cs] = m_new

    @pl.when(qi == 0)
    def _():
        scores(0, 0, True)

    @pl.when(qi > 0)
    def _():
        scores(0, 0, False)

    n_pair = lax.shift_right_logical(jnp.maximum(qi - 1, 0), 1)

    def pair(t, carry):
        kb = 2 * t
        scores(kb + 1, 1, False)
        absorb(kb, 0)
        scores(kb + 2, 0, False)
        absorb(kb + 1, 1)
        return carry

    lax.fori_loop(0, n_pair, pair, 0)
    left = qi - 2 * n_pair

    @pl.when(left == 0)
    def _():
        absorb(qi, 0)

    @pl.when(left == 1)
    def _():
        scores(qi, 1, True)
        absorb(qi - 1, 0)
        absorb(qi, 1)

    @pl.when(left == 2)
    def _():
        scores(qi - 1, 1, False)
        absorb(qi - 2, 0)
        scores(qi, 0, True)
        absorb(qi - 1, 1)
        absorb(qi, 0)

    lam = _lam(lq1, lk1, lq2, lk2)
    for hh in range(n_head):
        o = acc_ref[hh, :HEAD_W, :] / acc_ref[hh, HEAD_W:HEAD_W + 1, :]
        d = o[:, :tq] - lam * o[:, tq:]
        dn = d * lax.rsqrt(jnp.mean(d * d, axis=0, keepdims=True) + EPS)
        o_ref[0, :, lanes(hh)] = (dn.T * g_ref[...] * (1.0 - LAM_INIT)).astype(o_ref.dtype)


def _prompt_attn(q, k, v, lams, g_subln, tq, n_chain, n_head):
    b, s, _ = q.shape
    lam_spec = pl.BlockSpec((1, HALF_W), lambda bi, h, qi: (0, 0))
    qspec = pl.BlockSpec((1, tq, n_head * HEAD_W), lambda bi, h, qi: (bi, qi, h))
    kvspec = pl.BlockSpec((1, s, n_head * HEAD_W), lambda bi, h, qi: (bi, 0, h))
    return pl.pallas_call(
        functools.partial(_prompt_attn_kernel, tq=tq, n_chain=n_chain, n_head=n_head),
        grid=(b, N_HEADS // n_head, s // tq),
        in_specs=[lam_spec] * 4 + [pl.BlockSpec((1, HEAD_W), lambda bi, h, qi: (0, 0)),
                                   qspec, kvspec, kvspec],
        out_specs=qspec,
        out_shape=jax.ShapeDtypeStruct((b, s, GROUP_W), BF16),
        scratch_shapes=[pltpu.VMEM((n_head, HEAD_W, 2 * tq), BF16),
                        pltpu.VMEM((n_head, s // tq, HEAD_W + ONES_PAD, tq), BF16),
                        pltpu.VMEM((n_head, 1, 2 * tq), F32),
                        pltpu.VMEM((n_head, HEAD_W + ONES_PAD, 2 * tq), F32),
                        pltpu.VMEM((n_head, 2, n_chain, tq, 2 * tq // n_chain), F32)],
        compiler_params=_params(("arbitrary", "arbitrary", "arbitrary")),
        name="prompt_attn",
    )(*lams, g_subln, q, k, v)


def _sample_attn_kernel(pt_ref, lq1, lk1, lq2, lk2, g_ref, q_ref, kn_ref, vn_ref, *rest,
                        pages, t_new):
    k_refs = rest[:pages]
    v_refs = rest[pages:2 * pages]
    o_ref = rest[2 * pages]
    qbd_ref, m_ref, l_ref, acc_ref = rest[2 * pages + 1:]
    c = pl.program_id(1)
    nc = pl.num_programs(1)
    n_rows = qbd_ref.shape[0]

    @pl.when(c == 0)
    def _():
        q = q_ref[...].astype(F32)
        qt = jnp.concatenate([q] * (n_rows // t_new), axis=0)
        r = lax.broadcasted_iota(jnp.int32, qt.shape, 0)
        ln = lax.broadcasted_iota(jnp.int32, qt.shape, 1)
        qbd_ref[...] = jnp.where((r // t_new) == (ln // HALF_W), qt, 0.0).astype(BF16)
        m_ref[...] = jnp.full(m_ref.shape, -jnp.inf, F32)
        l_ref[...] = jnp.zeros(l_ref.shape, F32)
        acc_ref[...] = jnp.zeros(acc_ref.shape, F32)

    qbd = qbd_ref[...]

    def online(s_blocks, v_blocks):
        s = jnp.concatenate(s_blocks, axis=1) if len(s_blocks) > 1 else s_blocks[0]
        m_old = m_ref[...]
        m_new = jnp.maximum(m_old, jnp.max(s, axis=-1, keepdims=True))
        alpha = jnp.exp2(m_old - m_new)
        p = jnp.exp2(s - m_new)
        l_ref[...] = alpha * l_ref[...] + jnp.sum(p, axis=-1, keepdims=True)
        pv = None
        off = 0
        for vb in v_blocks:
            n = vb.shape[0]
            d = jnp.dot(p[:, off:off + n].astype(BF16), vb, preferred_element_type=F32)
            pv = d if pv is None else pv + d
            off += n
        acc_ref[...] = alpha * acc_ref[...] + pv
        m_ref[...] = m_new

    def scores(kb):
        return lax.dot_general(qbd, kb, (((1,), (1,)), ((), ())), preferred_element_type=F32)

    def load_page(ref):
        return jnp.concatenate([ref[pl.ds(h, PAGE, stride=N_HEADS), :] for h in range(N_HEADS)],
                               axis=1).astype(BF16)

    online([scores(load_page(r)) for r in k_refs], [load_page(r) for r in v_refs])

    @pl.when(c == nc - 1)
    def _():
        pad = jnp.zeros((PAGE - t_new, GROUP_W), F32)
        kn = jnp.concatenate([kn_ref[...], pad], axis=0).astype(BF16)
        vn = jnp.concatenate([vn_ref[...], pad], axis=0).astype(BF16)
        s = scores(kn)
        r = lax.broadcasted_iota(jnp.int32, s.shape, 0)
        col = lax.broadcasted_iota(jnp.int32, s.shape, 1)
        s = jnp.where(col <= (r % t_new), s, -jnp.inf)
        online([s], [vn])

        lam = _lam(lq1, lk1, lq2, lk2)
        o = acc_ref[...] / l_ref[...]
        for h in range(N_HEADS):
            blk = o[2 * t_new * h:2 * t_new * (h + 1), h * HEAD_W:(h + 1) * HEAD_W]
            o_ref[:, h * HEAD_W:(h + 1) * HEAD_W] = _subln(blk, lam, g_ref[...])


def _sample_attn(q, k_new, v_new, cache_k, cache_v, page_table, lams, g_subln, t_new, pages):
    n_b, n_pages = page_table.shape
    ck = cache_k.reshape(-1, HEAD_W)
    cv = cache_v.reshape(-1, HEAD_W)
    pt = page_table.reshape(-1)
    n_rows = 2 * N_HEADS * t_new
    const = lambda shape: pl.BlockSpec(shape, lambda b, c, pt_ref: (0, 0))
    rows = pl.BlockSpec((t_new, GROUP_W), lambda b, c, pt_ref: (b, 0))

    def page_spec(j):
        return pl.BlockSpec(
            (PAGE * N_HEADS, HEAD_W),
            lambda b, c, pt_ref: (pt_ref[b * n_pages + c * pages + j], 0))

    grid_spec = pltpu.PrefetchScalarGridSpec(
        num_scalar_prefetch=1,
        grid=(n_b, n_pages // pages),
        in_specs=[const((1, HALF_W))] * 4 + [const((1, HEAD_W)), rows, rows, rows]
                 + [page_spec(j) for j in range(pages)] * 2,
        out_specs=rows,
        scratch_shapes=[pltpu.VMEM((n_rows, GROUP_W), BF16),
                        pltpu.VMEM((n_rows, 1), F32),
                        pltpu.VMEM((n_rows, 1), F32),
                        pltpu.VMEM((n_rows, GROUP_W), F32)])
    return pl.pallas_call(
        functools.partial(_sample_attn_kernel, pages=pages, t_new=t_new),
        grid_spec=grid_spec,
        out_shape=jax.ShapeDtypeStruct((n_b * t_new, GROUP_W), F32),
        compiler_params=_params(("arbitrary", "arbitrary")),
        name="sample_attn",
    )(pt, *lams, g_subln, q, k_new, v_new, *([ck] * pages), *([cv] * pages))


OUT_PROJ_SUB = 128


def _out_proj_kernel(a_ref, b_ref, w_ref, x_ref, gpost_ref, gpre_ref, h_ref, hn_ref, *wb_refs):
    if wb_refs:
        wb_ref, = wb_refs
        wb_ref[...] = w_ref[...].astype(BF16)
    else:
        wb_ref = w_ref
    tm = x_ref.shape[0]
    sub = min(tm, OUT_PROJ_SUB)
    for r in range(tm // sub):
        rs = slice(r * sub, (r + 1) * sub)
        mix = jnp.dot(a_ref[rs, :].astype(BF16), wb_ref[:GROUP_W, :], preferred_element_type=F32)
        mix += jnp.dot(b_ref[rs, :].astype(BF16), wb_ref[GROUP_W:, :],
                       preferred_element_type=F32)
        h = x_ref[rs, :] + _rms(mix, gpost_ref[...])
        h_ref[rs, :] = h
        hn_ref[rs, :] = _rms(h, gpre_ref[...]).astype(hn_ref.dtype)


def _out_proj(a, b, w, x2d, g_post, g_pre, tm, cast_w):
    m, d = x2d.shape
    row = lambda i: (i, 0)
    const = lambda i: (0, 0)
    out_specs = [pl.BlockSpec((tm, d), row), pl.BlockSpec((tm, d), row)]
    out_shape = [jax.ShapeDtypeStruct((m, d), F32), jax.ShapeDtypeStruct((m, d), BF16)]
    if cast_w:
        assert m == tm, "the bf16 weight copy is written once"
        out_specs.append(pl.BlockSpec(w.shape, const))
        out_shape.append(jax.ShapeDtypeStruct(w.shape, BF16))
    return pl.pallas_call(
        _out_proj_kernel,
        grid=(m // tm,),
        in_specs=[pl.BlockSpec((tm, GROUP_W), row), pl.BlockSpec((tm, GROUP_W), row),
                  pl.BlockSpec(w.shape, const), pl.BlockSpec((tm, d), row),
                  pl.BlockSpec((1, d), const), pl.BlockSpec((1, d), const)],
        out_specs=out_specs,
        out_shape=out_shape,
        compiler_params=_params(("arbitrary",)),
        name="out_proj",
    )(a, b, w, x2d, g_post, g_pre)


PRE = 8
FFN_SUB = 128


def _gated(cg, cv):
    return (jax.nn.gelu(cg, approximate=True) * cv).astype(BF16)


def _ffn_finish(acc, h_ref, g_ref, o_ref):
    o_ref[...] = h_ref[...] + _rms(acc, g_ref[...])


def _prompt_ffn_kernel(hn_ref, wg_ref, wv_ref, cwg_ref, cwv_ref, cbg_ref, cbv_ref,
                       wd_ref, h_ref, g_ref, o_ref, lastg_ref, lastv_ref,
                       ug_ref, uv_ref, carryg_ref, carryv_ref, acc_ref, *, tm, tiles_per_seq):
    i = pl.program_id(0)
    f = pl.program_id(1)
    nf = pl.num_programs(1)
    nh = CONV_W - 1

    @pl.when((i == 0) & (f == 0))
    def _():
        carryg_ref[...] = jnp.zeros(carryg_ref.shape, F32)
        carryv_ref[...] = jnp.zeros(carryv_ref.shape, F32)

    @pl.when(f == 0)
    def _():
        acc_ref[...] = jnp.zeros(acc_ref.shape, F32)

    def conv(u_ref, cw_ref, cb_ref):
        out = cb_ref[...]
        for j in range(CONV_W):
            lo = PRE - nh + j
            out = out + u_ref[lo:lo + tm, :] * cw_ref[j:j + 1, :]
        return out

    def step(last):
        hn = hn_ref[...]
        starts = (i % tiles_per_seq) == 0
        for u_ref, w_ref, carry_ref, last_ref in ((ug_ref, wg_ref, carryg_ref, lastg_ref),
                                                  (uv_ref, wv_ref, carryv_ref, lastv_ref)):
            prev = carry_ref[f]
            u_ref[PRE - nh:PRE, :] = jnp.where(starts, jnp.zeros_like(prev), prev)
            u_ref[PRE:, :] = jnp.dot(hn, w_ref[...], preferred_element_type=F32)
            tail = u_ref[PRE + tm - nh:, :]
            carry_ref[f] = tail
            last_ref[0] = tail
        act = _gated(conv(ug_ref, cwg_ref, cbg_ref), conv(uv_ref, cwv_ref, cbv_ref))
        if not last:
            acc_ref[...] += jnp.dot(act, wd_ref[...], preferred_element_type=F32)
        else:
            for r in range(tm // FFN_SUB):
                rs = slice(r * FFN_SUB, (r + 1) * FFN_SUB)
                y = acc_ref[rs, :] + jnp.dot(act[rs, :], wd_ref[...], preferred_element_type=F32)
                o_ref[rs, :] = h_ref[rs, :] + _rms(y, g_ref[...])

    pl.when(f < nf - 1)(functools.partial(step, False))
    pl.when(f == nf - 1)(functools.partial(step, True))


def _prompt_ffn(hn, h, w_gate, w_val, w_conv, b_conv, w_down, g_post, tm, tf, seq):
    m, d = h.shape
    d_ff = w_down.shape[0]
    nf = d_ff // tf
    nt = m // tm
    row = lambda i, f: (i, 0)
    const = lambda i, f: (0, 0)
    gate = lambda i, f: (0, f)
    val = lambda i, f: (0, nf + f)
    last = pl.BlockSpec((1, CONV_W - 1, tf), lambda i, f: (i, 0, f))
    return pl.pallas_call(
        functools.partial(_prompt_ffn_kernel, tm=tm, tiles_per_seq=seq // tm),
        grid=(nt, nf),
        in_specs=[pl.BlockSpec((tm, d), row),
                  pl.BlockSpec((d, tf), gate), pl.BlockSpec((d, tf), gate),
                  pl.BlockSpec((CONV_W, tf), gate), pl.BlockSpec((CONV_W, tf), val),
                  pl.BlockSpec((1, tf), gate), pl.BlockSpec((1, tf), val),
                  pl.BlockSpec((tf, d), lambda i, f: (f, 0)),
                  pl.BlockSpec((tm, d), row),
                  pl.BlockSpec((1, d), const)],
        out_specs=[pl.BlockSpec((tm, d), row), last, last],
        out_shape=[jax.ShapeDtypeStruct((m, d), F32),
                   jax.ShapeDtypeStruct((nt, CONV_W - 1, d_ff), F32),
                   jax.ShapeDtypeStruct((nt, CONV_W - 1, d_ff), F32)],
        scratch_shapes=[pltpu.VMEM((PRE + tm, tf), F32),
                        pltpu.VMEM((PRE + tm, tf), F32),
                        pltpu.VMEM((nf, CONV_W - 1, tf), F32),
                        pltpu.VMEM((nf, CONV_W - 1, tf), F32),
                        pltpu.VMEM((tm, d), F32)],
        compiler_params=_params(("arbitrary", "arbitrary")),
        name="prompt_ffn",
    )(hn, w_gate, w_val, w_conv, w_conv, b_conv, b_conv, w_down, h, g_post)


def _sample_ffn_kernel(hn_ref, histg_ref, histv_ref, wg_ref, wv_ref, cwg_ref, cwv_ref,
                       cbg_ref, cbv_ref, wd_ref, h_ref, g_ref, o_ref, newg_ref, newv_ref,
                       wgb_ref, wvb_ref, wdb_ref, acc_ref, hpg_ref, hpv_ref, *, n_b, t_new):
    f = pl.program_id(0)
    nh = CONV_W - 1

    @pl.when(f == 0)
    def _():
        acc_ref[...] = jnp.zeros(acc_ref.shape, F32)

    wgb_ref[...] = wg_ref[...].astype(BF16)
    wvb_ref[...] = wv_ref[...].astype(BF16)
    wdb_ref[...] = wd_ref[...].astype(BF16)
    hn = hn_ref[...]
    ug = jnp.dot(hn, wgb_ref[...], preferred_element_type=F32)
    uv = jnp.dot(hn, wvb_ref[...], preferred_element_type=F32)

    def conv(u, hist_ref, cw_ref, cb_ref, new_ref, hp_ref):
        outs = []
        for b in range(n_b):
            hp_ref[b, t_new - nh:t_new, :] = hist_ref[b]
            hp_ref[b, t_new:, :] = u[b * t_new:(b + 1) * t_new, :]
            c = cb_ref[...]
            for j in range(CONV_W):
                lo = t_new - nh + j
                c = c + hp_ref[b, lo:lo + t_new, :] * cw_ref[j:j + 1, :]
            outs.append(c)
            new_ref[b] = hp_ref[b, 2 * t_new - nh:, :]
        return jnp.concatenate(outs, axis=0)

    act = _gated(conv(ug, histg_ref, cwg_ref, cbg_ref, newg_ref, hpg_ref),
                 conv(uv, histv_ref, cwv_ref, cbv_ref, newv_ref, hpv_ref))
    acc_ref[...] += jnp.dot(act, wdb_ref[...], preferred_element_type=F32)

    @pl.when(f == pl.num_programs(0) - 1)
    def _():
        _ffn_finish(acc_ref[...], h_ref, g_ref, o_ref)


def _sample_ffn(hn, h, hist, w_up, w_conv, b_conv, w_down, g_post, tf, n_b, t_new):
    m, d = h.shape
    d_ff = w_down.shape[0]
    nf = d_ff // tf
    const = lambda f: (0, 0)
    gate = lambda f: (0, f)
    val = lambda f: (0, nf + f)
    hist_g = pl.BlockSpec((n_b, CONV_W - 1, tf), lambda f: (0, 0, f))
    hist_v = pl.BlockSpec((n_b, CONV_W - 1, tf), lambda f: (0, 0, nf + f))
    return pl.pallas_call(
        functools.partial(_sample_ffn_kernel, n_b=n_b, t_new=t_new),
        grid=(nf,),
        in_specs=[pl.BlockSpec((m, d), const), hist_g, hist_v,
                  pl.BlockSpec((d, tf), gate), pl.BlockSpec((d, tf), val),
                  pl.BlockSpec((CONV_W, tf), gate), pl.BlockSpec((CONV_W, tf), val),
                  pl.BlockSpec((1, tf), gate), pl.BlockSpec((1, tf), val),
                  pl.BlockSpec((tf, d), lambda f: (f, 0)),
                  pl.BlockSpec((m, d), const), pl.BlockSpec((1, d), const)],
        out_specs=[pl.BlockSpec((m, d), const), hist_g, hist_g,
                   pl.BlockSpec((d, tf), gate), pl.BlockSpec((d, tf), gate),
                   pl.BlockSpec((tf, d), lambda f: (f, 0))],
        out_shape=[jax.ShapeDtypeStruct((m, d), F32),
                   jax.ShapeDtypeStruct((n_b, CONV_W - 1, d_ff), F32),
                   jax.ShapeDtypeStruct((n_b, CONV_W - 1, d_ff), F32),
                   jax.ShapeDtypeStruct((d, d_ff), BF16),
                   jax.ShapeDtypeStruct((d, d_ff), BF16),
                   jax.ShapeDtypeStruct((d_ff, d), BF16)],
        scratch_shapes=[pltpu.VMEM((m, d), F32),
                        pltpu.VMEM((n_b, 2 * t_new, tf), F32),
                        pltpu.VMEM((n_b, 2 * t_new, tf), F32)],
        compiler_params=_params(("arbitrary",)),
        name="sample_ffn",
    )(hn, hist, hist, w_up, w_up, w_conv, w_conv, b_conv, b_conv, w_down, h, g_post)


def kernel(x_prompt, x_sample, cache_k, cache_v, state_conv, page_table, g_pre_mix, w_in, g_v_a,
           w_s, b_s, lam_q1, lam_k1, lam_q2, lam_k2, g_subln, w_out, g_post_mix, g_pre_ffn, w_up,
           w_conv, b_conv, w_down, g_post_ffn):
    n_bp, seq, d = x_prompt.shape
    n_bs, t_new, _ = x_sample.shape
    depth = w_in.shape[0]
    assert depth == 1

    b_t = b_s[0].T
    lams = (lam_q1, lam_k1, lam_q2, lam_k2)
    ms = n_bs * t_new
    xp = x_prompt.reshape(n_bp * seq, d)
    xs = x_sample.reshape(ms, d)

    a_s, vas, qs, ks, vs, _, _, w_in_b = _in_proj(
        xs, g_pre_mix, w_in[0], g_v_a[0], w_s[0], b_t, ms, t_new, F32, True, True, False)
    a_out, q, k, v, kb, vb = _in_proj(
        xp, g_pre_mix, w_in_b, g_v_a[0], w_s[0], b_t, 512, CHUNK, BF16, False, False, True)

    b_sm = _sample_attn(qs, ks, vs, cache_k, cache_v, page_table, lams, g_subln, t_new, 16)
    sh3 = (n_bp, seq, GROUP_W)
    b_out = _prompt_attn(q.reshape(sh3), kb.reshape(sh3), vb.reshape(sh3), lams, g_subln, 512, 4, 4)

    hs, hns, w_out_b = _out_proj(a_s, b_sm, w_out[0], xs, g_post_mix, g_pre_ffn, ms, True)
    h, hn = _out_proj(a_out, b_out.reshape(n_bp * seq, GROUP_W), w_out_b, xp,
                      g_post_mix, g_pre_ffn, 512, False)

    ys, newg, newv, w_gate_b, w_val_b, w_down_b = _sample_ffn(
        hns, hs, state_conv[0], w_up[0], w_conv[0], b_conv, w_down[0], g_post_ffn, 512, n_bs, t_new)
    conv_s = jnp.concatenate([newg, newv], axis=-1)
    tm_ffn = 512
    yp, lastg, lastv = _prompt_ffn(hn, h, w_gate_b, w_val_b, w_conv[0], b_conv, w_down_b,
                                   g_post_ffn, tm_ffn, 512, seq)
    tps = seq // tm_ffn
    conv_p = jnp.concatenate([lastg[tps - 1::tps], lastv[tps - 1::tps]], axis=-1)

    hsh = (depth, n_bp, seq, N_HEADS, HEAD_W)
    ssh = (depth, n_bs, t_new, N_HEADS, HEAD_W)
    return (yp.reshape(n_bp, seq, d), ys.reshape(n_bs, t_new, d),
            k.reshape(hsh), v.reshape(hsh), conv_p[None],
            ks.reshape(ssh), vs.reshape(ssh), conv_s[None], vas.reshape(ssh))
```

```python
import functools
import math

import jax
import jax.numpy as jnp
from jax import lax
from jax.experimental import pallas as pl
from jax.experimental.pallas import tpu as pltpu

F32 = jnp.float32
BF16 = jnp.bfloat16

EPS = 1e-6
HEAD_W = 128
HALF_W = HEAD_W // 2
N_HEADS = 8
GROUP_W = N_HEADS * HEAD_W
CHUNK = 128
PAGE = 128
CONV_W = 3
LAM_INIT = 0.8 - 0.6 * math.exp(0.0)
Q_SCALE = HALF_W ** -0.5 * math.log2(math.e)
VMEM_LIMIT = 56 * 1024 * 1024


def _params(sem):
    return pltpu.CompilerParams(dimension_semantics=sem, vmem_limit_bytes=VMEM_LIMIT)


def _rms(x, g):
    return x * lax.rsqrt(jnp.mean(x * x, axis=-1, keepdims=True) + EPS) * g


def _lam(lq1, lk1, lq2, lk2):
    a = jnp.sum(lq1[...] * lk1[...], axis=-1, keepdims=True)
    b = jnp.sum(lq2[...] * lk2[...], axis=-1, keepdims=True)
    return jnp.exp(a) - jnp.exp(b) + LAM_INIT


MXU_N = 512


def _tril_weights(ws_ref, h):
    r = lax.broadcasted_iota(jnp.int32, (CHUNK, CHUNK), 0)
    c = lax.broadcasted_iota(jnp.int32, (CHUNK, CHUNK), 1)
    return jnp.where(c <= r, ws_ref[h], 0.0).astype(BF16)


N_GROUPS = 5


def _in_proj_kernel(x_ref, g_ref, w_ref, gva_ref, ws_ref, bt_ref, *rest, rows, emit_va, cast_w,
                    resident):
    rest = list(rest)
    a_ref = rest.pop(0)
    va_ref = rest.pop(0) if emit_va else None
    q_ref, k_ref, v_ref, kb_ref, vb_ref = rest[:5]
    rest = rest[5:]
    wb_ref = rest.pop(0) if cast_w else w_ref
    xg_ref, r_ref, u_ref = rest
    tm = xg_ref.shape[0]
    hpc = MXU_N // HEAD_W

    if cast_w:
        wb_ref[...] = w_ref[...].astype(BF16)

    def project(group, c):
        col = (group * GROUP_W if resident else 0) + c * MXU_N
        z = jnp.dot(xg_ref[...], wb_ref[:, col:col + MXU_N], preferred_element_type=F32)
        return z * r_ref[...]

    def chunks():
        return [(c, slice(c * MXU_N, (c + 1) * MXU_N)) for c in range(GROUP_W // MXU_N)]

    def group_u():
        x = x_ref[...]
        xg_ref[...] = (x * g_ref[...]).astype(BF16)
        r_ref[...] = lax.rsqrt(jnp.mean(x * x, axis=-1, keepdims=True) + EPS)
        for c, cs in chunks():
            u_ref[:, cs] = jax.nn.gelu(project(0, c), approximate=True).astype(u_ref.dtype)

    def group_va():
        for c, _ in chunks():
            a = jax.nn.gelu(project(1, c), approximate=True)
            for hh in range(hpc):
                h = c * hpc + hh
                sl = slice(h * HEAD_W, (h + 1) * HEAD_W)
                va = _rms(a[:, hh * HEAD_W:(hh + 1) * HEAD_W], gva_ref[h:h + 1, :])
                if emit_va:
                    va_ref[:, sl] = va
                w = _tril_weights(ws_ref, h)
                bias = bt_ref[:, h:h + 1]
                for t in range(tm // rows):
                    rs = slice(t * rows, (t + 1) * rows)
                    vc = va[rs]
                    if rows < CHUNK:
                        vc = jnp.concatenate([vc, jnp.zeros((CHUNK - rows, HEAD_W), F32)], axis=0)
                    mixed = jnp.dot(w, vc.astype(BF16), preferred_element_type=F32) + bias
                    a_ref[rs, sl] = (u_ref[rs, sl].astype(F32) * mixed[:rows]).astype(a_ref.dtype)

    def group_q():
        for c, cs in chunks():
            q_ref[:, cs] = (project(2, c) * Q_SCALE).astype(q_ref.dtype)

    def group_kv(group, f32_ref, bf16_ref):
        for c, cs in chunks():
            z = project(group, c)
            f32_ref[:, cs] = z
            bf16_ref[:, cs] = z.astype(BF16)

    groups = [group_u, group_va, group_q,
              functools.partial(group_kv, 3, k_ref, kb_ref),
              functools.partial(group_kv, 4, v_ref, vb_ref)]
    if resident:
        for run in groups:
            run()
    else:
        for j, run in enumerate(groups):
            pl.when(pl.program_id(1) == j)(run)


def _in_proj(x2d, g, w, gva, w_s, b_t, tm, rows, act_dtype, emit_va, cast_w, resident):
    m, d = x2d.shape
    row = lambda i, j: (i, 0)
    const = lambda i, j: (0, 0)
    blk = pl.BlockSpec((tm, GROUP_W), row)
    if resident:
        wspec = pl.BlockSpec(w.shape, const, pipeline_mode=pl.Buffered(1))
    else:
        wspec = pl.BlockSpec((d, GROUP_W), lambda i, j: (0, j))
    act = jax.ShapeDtypeStruct((m, GROUP_W), act_dtype)
    out_shape = ([act] + ([jax.ShapeDtypeStruct((m, GROUP_W), F32)] if emit_va else [])
                 + [act, jax.ShapeDtypeStruct((m, GROUP_W), F32),
                    jax.ShapeDtypeStruct((m, GROUP_W), F32),
                    jax.ShapeDtypeStruct((m, GROUP_W), BF16),
                    jax.ShapeDtypeStruct((m, GROUP_W), BF16)])
    out_specs = [blk] * len(out_shape)
    if cast_w:
        assert m == tm and not resident, "the bf16 weight copy is written once per column group"
        out_shape.append(jax.ShapeDtypeStruct(w.shape, BF16))
        out_specs.append(wspec)
    return pl.pallas_call(
        functools.partial(_in_proj_kernel, rows=rows, emit_va=emit_va, cast_w=cast_w,
                          resident=resident),
        grid=(m // tm, 1 if resident else N_GROUPS),
        in_specs=[pl.BlockSpec((tm, d), row), pl.BlockSpec((1, d), const), wspec,
                  pl.BlockSpec((N_HEADS, HEAD_W), const),
                  pl.BlockSpec((N_HEADS, CHUNK, CHUNK), lambda i, j: (0, 0, 0)),
                  pl.BlockSpec((CHUNK, N_HEADS), const)],
        out_specs=out_specs,
        out_shape=out_shape,
        scratch_shapes=[pltpu.VMEM((tm, d), BF16), pltpu.VMEM((tm, 1), F32),
                        pltpu.VMEM((tm, GROUP_W), act_dtype)],
        compiler_params=_params(("arbitrary", "arbitrary")),
        name="in_proj",
    )(x2d, g, w, gva, w_s, b_t)


ONES_PAD = 16


def _subln(o, lam, g):
    t = o.shape[0] // 2
    d = o[:t] - lam * o[t:]
    return _rms(d, g) * (1.0 - LAM_INIT)


def _prompt_attn_kernel(lq1, lk1, lq2, lk2, g_ref, q_ref, k_ref, v_ref, o_ref,
                        qt_ref, vt_ref, m_ref, acc_ref, s_ref, *, tq, n_chain, n_head):
    qi = pl.program_id(2)

    def lanes(hh):
        return slice(hh * HEAD_W, (hh + 1) * HEAD_W)

    @pl.when(qi == 0)
    def _():
        ones_row = lax.broadcasted_iota(jnp.int32, (ONES_PAD, tq), 0) == 0
        for hh in range(n_head):
            for c in range(vt_ref.shape[1]):
                vt_ref[hh, c, :HEAD_W, :] = v_ref[0, c * tq:(c + 1) * tq, lanes(hh)].astype(
                    F32).T.astype(BF16)
                vt_ref[hh, c, HEAD_W:, :] = jnp.where(ones_row, 1.0, 0.0).astype(BF16)

    for hh in range(n_head):
        qf = q_ref[0, :, lanes(hh)].astype(F32).T
        dim = lax.broadcasted_iota(jnp.int32, qf.shape, 0)
        qt_ref[hh, :, :tq] = jnp.where(dim < HALF_W, qf, 0.0).astype(BF16)
        qt_ref[hh, :, tq:] = jnp.where(dim >= HALF_W, qf, 0.0).astype(BF16)
    m_ref[...] = jnp.full(m_ref.shape, -jnp.inf, F32)
    acc_ref[...] = jnp.zeros(acc_ref.shape, F32)
    cw = 2 * tq // n_chain

    def n_keys(c, diag):
        return min(tq, (c * cw) % tq + cw) if diag else tq

    def scores(kb, slot, diag):
        start = pl.multiple_of(kb * tq, tq)
        for hh in range(n_head):
            k = k_ref[0, pl.ds(start, tq), lanes(hh)]
            for c in range(n_chain):
                cs = slice(c * cw, (c + 1) * cw)
                nk = n_keys(c, diag)
                s = jnp.dot(k[:nk], qt_ref[hh, :, cs], preferred_element_type=F32)
                if diag:
                    key = lax.broadcasted_iota(jnp.int32, s.shape, 0)
                    qry = lax.broadcasted_iota(jnp.int32, s.shape, 1) + (c * cw) % tq
                    s = jnp.where(key <= qry, s, -jnp.inf)
                s_ref[hh, slot, c, :nk, :] = s

    def absorb(kb, slot, diag=False):
        for hh in range(n_head):
            vt = vt_ref[hh, kb]
            for c in range(n_chain):
                cs = slice(c * cw, (c + 1) * cw)
                nk = n_keys(c, diag)
                s = s_ref[hh, slot, c, :nk, :]
                m_old = m_ref[hh, :, cs]
                m_new = jnp.maximum(m_old, jnp.max(s, axis=0, keepdims=True))
                alpha = jnp.exp2(m_old - m_new)
                p = jnp.exp2(s - m_new)
                acc_ref[hh, :, cs] = alpha * acc_ref[hh, :, cs] + jnp.dot(
                    vt[:, :nk], p.astype(BF16), preferred_element_type=F32)
                m_ref[hh, :, cs] = m_new

    @pl.when(qi == 0)
    def _():
        scores(0, 0, True)

    @pl.when(qi > 0)
    def _():
        scores(0, 0, False)

    n_pair = lax.shift_right_logical(jnp.maximum(qi - 1, 0), 1)

    def pair(t, carry):
        kb = 2 * t
        scores(kb + 1, 1, False)
        absorb(kb, 0)
        scores(kb + 2, 0, False)
        absorb(kb + 1, 1)
        return carry

    lax.fori_loop(0, n_pair, pair, 0)
    left = qi - 2 * n_pair

    @pl.when(left == 0)
    def _():
        absorb(qi, 0, True)

    @pl.when(left == 1)
    def _():
        scores(qi, 1, True)
        absorb(qi - 1, 0)
        absorb(qi, 1, True)

    @pl.when(left == 2)
    def _():
        scores(qi - 1, 1, False)
        absorb(qi - 2, 0)
        scores(qi, 0, True)
        absorb(qi - 1, 1)
        absorb(qi, 0, True)

    lam = _lam(lq1, lk1, lq2, lk2)
    for hh in range(n_head):
        o = acc_ref[hh, :HEAD_W, :] / acc_ref[hh, HEAD_W:HEAD_W + 1, :]
        d = o[:, :tq] - lam * o[:, tq:]
        dn = d * lax.rsqrt(jnp.mean(d * d, axis=0, keepdims=True) + EPS)
        o_ref[0, :, lanes(hh)] = (dn.T * g_ref[...] * (1.0 - LAM_INIT)).astype(o_ref.dtype)


def _prompt_attn(q, k, v, lams, g_subln, tq, n_chain, n_head):
    b, s, _ = q.shape
    lam_spec = pl.BlockSpec((1, HALF_W), lambda bi, h, qi: (0, 0))
    qspec = pl.BlockSpec((1, tq, n_head * HEAD_W), lambda bi, h, qi: (bi, qi, h))
    kvspec = pl.BlockSpec((1, s, n_head * HEAD_W), lambda bi, h, qi: (bi, 0, h))
    return pl.pallas_call(
        functools.partial(_prompt_attn_kernel, tq=tq, n_chain=n_chain, n_head=n_head),
        grid=(b, N_HEADS // n_head, s // tq),
        in_specs=[lam_spec] * 4 + [pl.BlockSpec((1, HEAD_W), lambda bi, h, qi: (0, 0)),
                                   qspec, kvspec, kvspec],
        out_specs=qspec,
        out_shape=jax.ShapeDtypeStruct((b, s, GROUP_W), BF16),
        scratch_shapes=[pltpu.VMEM((n_head, HEAD_W, 2 * tq), BF16),
                        pltpu.VMEM((n_head, s // tq, HEAD_W + ONES_PAD, tq), BF16),
                        pltpu.VMEM((n_head, 1, 2 * tq), F32),
                        pltpu.VMEM((n_head, HEAD_W + ONES_PAD, 2 * tq), F32),
                        pltpu.VMEM((n_head, 2, n_chain, tq, 2 * tq // n_chain), F32)],
        compiler_params=_params(("arbitrary", "arbitrary", "arbitrary")),
        name="prompt_attn",
    )(*lams, g_subln, q, k, v)


def _sample_attn_kernel(pt_ref, lq1, lk1, lq2, lk2, g_ref, q_ref, kn_ref, vn_ref, *rest,
                        pages, t_new):
    k_refs = rest[:pages]
    v_refs = rest[pages:2 * pages]
    o_ref = rest[2 * pages]
    qbd_ref, m_ref, l_ref, acc_ref = rest[2 * pages + 1:]
    c = pl.program_id(1)
    nc = pl.num_programs(1)
    n_rows = qbd_ref.shape[0]

    @pl.when(c == 0)
    def _():
        q = q_ref[...].astype(F32)
        qt = jnp.concatenate([q] * (n_rows // t_new), axis=0)
        r = lax.broadcasted_iota(jnp.int32, qt.shape, 0)
        ln = lax.broadcasted_iota(jnp.int32, qt.shape, 1)
        qbd_ref[...] = jnp.where((r // t_new) == (ln // HALF_W), qt, 0.0).astype(BF16)
        m_ref[...] = jnp.full(m_ref.shape, -jnp.inf, F32)
        l_ref[...] = jnp.zeros(l_ref.shape, F32)
        acc_ref[...] = jnp.zeros(acc_ref.shape, F32)

    qbd = qbd_ref[...]

    def online(s_blocks, v_blocks):
        s = jnp.concatenate(s_blocks, axis=1) if len(s_blocks) > 1 else s_blocks[0]
        m_old = m_ref[...]
        m_new = jnp.maximum(m_old, jnp.max(s, axis=-1, keepdims=True))
        alpha = jnp.exp2(m_old - m_new)
        p = jnp.exp2(s - m_new)
        l_ref[...] = alpha * l_ref[...] + jnp.sum(p, axis=-1, keepdims=True)
        pv = None
        off = 0
        for vb in v_blocks:
            n = vb.shape[0]
            d = jnp.dot(p[:, off:off + n].astype(BF16), vb, preferred_element_type=F32)
            pv = d if pv is None else pv + d
            off += n
        acc_ref[...] = alpha * acc_ref[...] + pv
        m_ref[...] = m_new

    def scores(kb):
        return lax.dot_general(qbd, kb, (((1,), (1,)), ((), ())), preferred_element_type=F32)

    def load_page(ref):
        return jnp.concatenate([ref[pl.ds(h, PAGE, stride=N_HEADS), :] for h in range(N_HEADS)],
                               axis=1).astype(BF16)

    online([scores(load_page(r)) for r in k_refs], [load_page(r) for r in v_refs])

    @pl.when(c == nc - 1)
    def _():
        pad = jnp.zeros((PAGE - t_new, GROUP_W), F32)
        kn = jnp.concatenate([kn_ref[...], pad], axis=0).astype(BF16)
        vn = jnp.concatenate([vn_ref[...], pad], axis=0).astype(BF16)
        s = scores(kn)
        r = lax.broadcasted_iota(jnp.int32, s.shape, 0)
        col = lax.broadcasted_iota(jnp.int32, s.shape, 1)
        s = jnp.where(col <= (r % t_new), s, -jnp.inf)
        online([s], [vn])

        lam = _lam(lq1, lk1, lq2, lk2)
        o = acc_ref[...] / l_ref[...]
        for h in range(N_HEADS):
            blk = o[2 * t_new * h:2 * t_new * (h + 1), h * HEAD_W:(h + 1) * HEAD_W]
            o_ref[:, h * HEAD_W:(h + 1) * HEAD_W] = _subln(blk, lam, g_ref[...])


def _sample_attn(q, k_new, v_new, cache_k, cache_v, page_table, lams, g_subln, t_new, pages):
    n_b, n_pages = page_table.shape
    ck = cache_k.reshape(-1, HEAD_W)
    cv = cache_v.reshape(-1, HEAD_W)
    pt = page_table.reshape(-1)
    n_rows = 2 * N_HEADS * t_new
    const = lambda shape: pl.BlockSpec(shape, lambda b, c, pt_ref: (0, 0))
    rows = pl.BlockSpec((t_new, GROUP_W), lambda b, c, pt_ref: (b, 0))

    def page_spec(j):
        return pl.BlockSpec(
            (PAGE * N_HEADS, HEAD_W),
            lambda b, c, pt_ref: (pt_ref[b * n_pages + c * pages + j], 0))

    grid_spec = pltpu.PrefetchScalarGridSpec(
        num_scalar_prefetch=1,
        grid=(n_b, n_pages // pages),
        in_specs=[const((1, HALF_W))] * 4 + [const((1, HEAD_W)), rows, rows, rows]
                 + [page_spec(j) for j in range(pages)] * 2,
        out_specs=rows,
        scratch_shapes=[pltpu.VMEM((n_rows, GROUP_W), BF16),
                        pltpu.VMEM((n_rows, 1), F32),
                        pltpu.VMEM((n_rows, 1), F32),
                        pltpu.VMEM((n_rows, GROUP_W), F32)])
    return pl.pallas_call(
        functools.partial(_sample_attn_kernel, pages=pages, t_new=t_new),
        grid_spec=grid_spec,
        out_shape=jax.ShapeDtypeStruct((n_b * t_new, GROUP_W), F32),
        compiler_params=_params(("arbitrary", "arbitrary")),
        name="sample_attn",
    )(pt, *lams, g_subln, q, k_new, v_new, *([ck] * pages), *([cv] * pages))


OUT_PROJ_SUB = 128


def _out_proj_kernel(a_ref, b_ref, w_ref, x_ref, gpost_ref, gpre_ref, h_ref, hn_ref, *wb_refs):
    if wb_refs:
        wb_ref, = wb_refs
        wb_ref[...] = w_ref[...].astype(BF16)
    else:
        wb_ref = w_ref
    tm = x_ref.shape[0]
    sub = min(tm, OUT_PROJ_SUB)
    for r in range(tm // sub):
        rs = slice(r * sub, (r + 1) * sub)
        mix = jnp.dot(a_ref[rs, :].astype(BF16), wb_ref[:GROUP_W, :], preferred_element_type=F32)
        mix += jnp.dot(b_ref[rs, :].astype(BF16), wb_ref[GROUP_W:, :],
                       preferred_element_type=F32)
        h = x_ref[rs, :] + _rms(mix, gpost_ref[...])
        h_ref[rs, :] = h
        hn_ref[rs, :] = _rms(h, gpre_ref[...]).astype(hn_ref.dtype)


def _out_proj(a, b, w, x2d, g_post, g_pre, tm, cast_w):
    m, d = x2d.shape
    row = lambda i: (i, 0)
    const = lambda i: (0, 0)
    out_specs = [pl.BlockSpec((tm, d), row), pl.BlockSpec((tm, d), row)]
    out_shape = [jax.ShapeDtypeStruct((m, d), F32), jax.ShapeDtypeStruct((m, d), BF16)]
    if cast_w:
        assert m == tm, "the bf16 weight copy is written once"
        out_specs.append(pl.BlockSpec(w.shape, const))
        out_shape.append(jax.ShapeDtypeStruct(w.shape, BF16))
    return pl.pallas_call(
        _out_proj_kernel,
        grid=(m // tm,),
        in_specs=[pl.BlockSpec((tm, GROUP_W), row), pl.BlockSpec((tm, GROUP_W), row),
                  pl.BlockSpec(w.shape, const), pl.BlockSpec((tm, d), row),
                  pl.BlockSpec((1, d), const), pl.BlockSpec((1, d), const)],
        out_specs=out_specs,
        out_shape=out_shape,
        compiler_params=_params(("arbitrary",)),
        name="out_proj",
    )(a, b, w, x2d, g_post, g_pre)


PRE = 8
FFN_SUB = 128


def _gated(cg, cv):
    return (jax.nn.gelu(cg, approximate=True) * cv).astype(BF16)


def _ffn_finish(acc, h_ref, g_ref, o_ref):
    o_ref[...] = h_ref[...] + _rms(acc, g_ref[...])


def _prompt_ffn_kernel(hn_ref, wg_ref, wv_ref, cwg_ref, cwv_ref, cbg_ref, cbv_ref,
                       wd_ref, h_ref, g_ref, o_ref, lastg_ref, lastv_ref,
                       ug_ref, uv_ref, carryg_ref, carryv_ref, acc_ref, *, tm, tiles_per_seq):
    i = pl.program_id(0)
    f = pl.program_id(1)
    nf = pl.num_programs(1)
    nh = CONV_W - 1

    @pl.when((i == 0) & (f == 0))
    def _():
        carryg_ref[...] = jnp.zeros(carryg_ref.shape, F32)
        carryv_ref[...] = jnp.zeros(carryv_ref.shape, F32)

    @pl.when(f == 0)
    def _():
        acc_ref[...] = jnp.zeros(acc_ref.shape, F32)

    def conv(u_ref, cw_ref, cb_ref):
        out = cb_ref[...]
        for j in range(CONV_W):
            lo = PRE - nh + j
            out = out + u_ref[lo:lo + tm, :] * cw_ref[j:j + 1, :]
        return out

    def step(last):
        hn = hn_ref[...]
        starts = (i % tiles_per_seq) == 0
        for u_ref, w_ref, carry_ref, last_ref in ((ug_ref, wg_ref, carryg_ref, lastg_ref),
                                                  (uv_ref, wv_ref, carryv_ref, lastv_ref)):
            prev = carry_ref[f]
            u_ref[PRE - nh:PRE, :] = jnp.where(starts, jnp.zeros_like(prev), prev)
            u_ref[PRE:, :] = jnp.dot(hn, w_ref[...], preferred_element_type=F32)
            tail = u_ref[PRE + tm - nh:, :]
            carry_ref[f] = tail
            last_ref[0] = tail
        act = _gated(conv(ug_ref, cwg_ref, cbg_ref), conv(uv_ref, cwv_ref, cbv_ref))
        if not last:
            acc_ref[...] += jnp.dot(act, wd_ref[...], preferred_element_type=F32)
        else:
            for r in range(tm // FFN_SUB):
                rs = slice(r * FFN_SUB, (r + 1) * FFN_SUB)
                y = acc_ref[rs, :] + jnp.dot(act[rs, :], wd_ref[...], preferred_element_type=F32)
                o_ref[rs, :] = h_ref[rs, :] + _rms(y, g_ref[...])

    pl.when(f < nf - 1)(functools.partial(step, False))
    pl.when(f == nf - 1)(functools.partial(step, True))


def _prompt_ffn(hn, h, w_gate, w_val, w_conv, b_conv, w_down, g_post, tm, tf, seq):
    m, d = h.shape
    d_ff = w_down.shape[0]
    nf = d_ff // tf
    nt = m // tm
    row = lambda i, f: (i, 0)
    const = lambda i, f: (0, 0)
    gate = lambda i, f: (0, f)
    val = lambda i, f: (0, nf + f)
    last = pl.BlockSpec((1, CONV_W - 1, tf), lambda i, f: (i, 0, f))
    return pl.pallas_call(
        functools.partial(_prompt_ffn_kernel, tm=tm, tiles_per_seq=seq // tm),
        grid=(nt, nf),
        in_specs=[pl.BlockSpec((tm, d), row),
                  pl.BlockSpec((d, tf), gate), pl.BlockSpec((d, tf), gate),
                  pl.BlockSpec((CONV_W, tf), gate), pl.BlockSpec((CONV_W, tf), val),
                  pl.BlockSpec((1, tf), gate), pl.BlockSpec((1, tf), val),
                  pl.BlockSpec((tf, d), lambda i, f: (f, 0)),
                  pl.BlockSpec((tm, d), row),
                  pl.BlockSpec((1, d), const)],
        out_specs=[pl.BlockSpec((tm, d), row), last, last],
        out_shape=[jax.ShapeDtypeStruct((m, d), F32),
                   jax.ShapeDtypeStruct((nt, CONV_W - 1, d_ff), F32),
                   jax.ShapeDtypeStruct((nt, CONV_W - 1, d_ff), F32)],
        scratch_shapes=[pltpu.VMEM((PRE + tm, tf), F32),
                        pltpu.VMEM((PRE + tm, tf), F32),
                        pltpu.VMEM((nf, CONV_W - 1, tf), F32),
                        pltpu.VMEM((nf, CONV_W - 1, tf), F32),
                        pltpu.VMEM((tm, d), F32)],
        compiler_params=_params(("arbitrary", "arbitrary")),
        name="prompt_ffn",
    )(hn, w_gate, w_val, w_conv, w_conv, b_conv, b_conv, w_down, h, g_post)


def _sample_ffn_kernel(hn_ref, histg_ref, histv_ref, wg_ref, wv_ref, cwg_ref, cwv_ref,
                       cbg_ref, cbv_ref, wd_ref, h_ref, g_ref, o_ref, newg_ref, newv_ref,
                       wgb_ref, wvb_ref, wdb_ref, acc_ref, hpg_ref, hpv_ref, *, n_b, t_new):
    f = pl.program_id(0)
    nh = CONV_W - 1

    @pl.when(f == 0)
    def _():
        acc_ref[...] = jnp.zeros(acc_ref.shape, F32)

    wgb_ref[...] = wg_ref[...].astype(BF16)
    wvb_ref[...] = wv_ref[...].astype(BF16)
    wdb_ref[...] = wd_ref[...].astype(BF16)
    hn = hn_ref[...]
    ug = jnp.dot(hn, wgb_ref[...], preferred_element_type=F32)
    uv = jnp.dot(hn, wvb_ref[...], preferred_element_type=F32)

    def conv(u, hist_ref, cw_ref, cb_ref, new_ref, hp_ref):
        outs = []
        for b in range(n_b):
            hp_ref[b, t_new - nh:t_new, :] = hist_ref[b]
            hp_ref[b, t_new:, :] = u[b * t_new:(b + 1) * t_new, :]
            c = cb_ref[...]
            for j in range(CONV_W):
                lo = t_new - nh + j
                c = c + hp_ref[b, lo:lo + t_new, :] * cw_ref[j:j + 1, :]
            outs.append(c)
            new_ref[b] = hp_ref[b, 2 * t_new - nh:, :]
        return jnp.concatenate(outs, axis=0)

    act = _gated(conv(ug, histg_ref, cwg_ref, cbg_ref, newg_ref, hpg_ref),
                 conv(uv, histv_ref, cwv_ref, cbv_ref, newv_ref, hpv_ref))
    acc_ref[...] += jnp.dot(act, wdb_ref[...], preferred_element_type=F32)

    @pl.when(f == pl.num_programs(0) - 1)
    def _():
        _ffn_finish(acc_ref[...], h_ref, g_ref, o_ref)


def _sample_ffn(hn, h, hist, w_up, w_conv, b_conv, w_down, g_post, tf, n_b, t_new):
    m, d = h.shape
    d_ff = w_down.shape[0]
    nf = d_ff // tf
    const = lambda f: (0, 0)
    gate = lambda f: (0, f)
    val = lambda f: (0, nf + f)
    hist_g = pl.BlockSpec((n_b, CONV_W - 1, tf), lambda f: (0, 0, f))
    hist_v = pl.BlockSpec((n_b, CONV_W - 1, tf), lambda f: (0, 0, nf + f))
    return pl.pallas_call(
        functools.partial(_sample_ffn_kernel, n_b=n_b, t_new=t_new),
        grid=(nf,),
        in_specs=[pl.BlockSpec((m, d), const), hist_g, hist_v,
                  pl.BlockSpec((d, tf), gate), pl.BlockSpec((d, tf), val),
                  pl.BlockSpec((CONV_W, tf), gate), pl.BlockSpec((CONV_W, tf), val),
                  pl.BlockSpec((1, tf), gate), pl.BlockSpec((1, tf), val),
                  pl.BlockSpec((tf, d), lambda f: (f, 0)),
                  pl.BlockSpec((m, d), const), pl.BlockSpec((1, d), const)],
        out_specs=[pl.BlockSpec((m, d), const), hist_g, hist_g,
                   pl.BlockSpec((d, tf), gate), pl.BlockSpec((d, tf), gate),
                   pl.BlockSpec((tf, d), lambda f: (f, 0))],
        out_shape=[jax.ShapeDtypeStruct((m, d), F32),
                   jax.ShapeDtypeStruct((n_b, CONV_W - 1, d_ff), F32),
                   jax.ShapeDtypeStruct((n_b, CONV_W - 1, d_ff), F32),
                   jax.ShapeDtypeStruct((d, d_ff), BF16),
                   jax.ShapeDtypeStruct((d, d_ff), BF16),
                   jax.ShapeDtypeStruct((d_ff, d), BF16)],
        scratch_shapes=[pltpu.VMEM((m, d), F32),
                        pltpu.VMEM((n_b, 2 * t_new, tf), F32),
                        pltpu.VMEM((n_b, 2 * t_new, tf), F32)],
        compiler_params=_params(("arbitrary",)),
        name="sample_ffn",
    )(hn, hist, hist, w_up, w_up, w_conv, w_conv, b_conv, b_conv, w_down, h, g_post)


def kernel(x_prompt, x_sample, cache_k, cache_v, state_conv, page_table, g_pre_mix, w_in, g_v_a,
           w_s, b_s, lam_q1, lam_k1, lam_q2, lam_k2, g_subln, w_out, g_post_mix, g_pre_ffn, w_up,
           w_conv, b_conv, w_down, g_post_ffn):
    n_bp, seq, d = x_prompt.shape
    n_bs, t_new, _ = x_sample.shape
    depth = w_in.shape[0]
    assert depth == 1

    b_t = b_s[0].T
    lams = (lam_q1, lam_k1, lam_q2, lam_k2)
    ms = n_bs * t_new
    xp = x_prompt.reshape(n_bp * seq, d)
    xs = x_sample.reshape(ms, d)

    a_s, vas, qs, ks, vs, _, _, w_in_b = _in_proj(
        xs, g_pre_mix, w_in[0], g_v_a[0], w_s[0], b_t, ms, t_new, F32, True, True, False)
    a_out, q, k, v, kb, vb = _in_proj(
        xp, g_pre_mix, w_in_b, g_v_a[0], w_s[0], b_t, 512, CHUNK, BF16, False, False, True)

    b_sm = _sample_attn(qs, ks, vs, cache_k, cache_v, page_table, lams, g_subln, t_new, 16)
    sh3 = (n_bp, seq, GROUP_W)
    b_out = _prompt_attn(q.reshape(sh3), kb.reshape(sh3), vb.reshape(sh3), lams, g_subln, 512, 4, 4)

    hs, hns, w_out_b = _out_proj(a_s, b_sm, w_out[0], xs, g_post_mix, g_pre_ffn, ms, True)
    h, hn = _out_proj(a_out, b_out.reshape(n_bp * seq, GROUP_W), w_out_b, xp,
                      g_post_mix, g_pre_ffn, 512, False)

    ys, newg, newv, w_gate_b, w_val_b, w_down_b = _sample_ffn(
        hns, hs, state_conv[0], w_up[0], w_conv[0], b_conv, w_down[0], g_post_ffn, 512, n_bs, t_new)
    conv_s = jnp.concatenate([newg, newv], axis=-1)
    tm_ffn = 512
    yp, lastg, lastv = _prompt_ffn(hn, h, w_gate_b, w_val_b, w_conv[0], b_conv, w_down_b,
                                   g_post_ffn, tm_ffn, 512, seq)
    tps = seq // tm_ffn
    conv_p = jnp.concatenate([lastg[tps - 1::tps], lastv[tps - 1::tps]], axis=-1)

    hsh = (depth, n_bp, seq, N_HEADS, HEAD_W)
    ssh = (depth, n_bs, t_new, N_HEADS, HEAD_W)
    return (yp.reshape(n_bp, seq, d), ys.reshape(n_bs, t_new, d),
            k.reshape(hsh), v.reshape(hsh), conv_p[None],
            ks.reshape(ssh), vs.reshape(ssh), conv_s[None], vas.reshape(ssh))
```

```python
import functools
import math

import jax
import jax.numpy as jnp
from jax import lax
from jax.experimental import pallas as pl
from jax.experimental.pallas import tpu as pltpu

F32 = jnp.float32
BF16 = jnp.bfloat16

EPS = 1e-6
HEAD_W = 128
HALF_W = HEAD_W // 2
N_HEADS = 8
GROUP_W = N_HEADS * HEAD_W
CHUNK = 128
PAGE = 128
CONV_W = 3
LAM_INIT = 0.8 - 0.6 * math.exp(0.0)
Q_SCALE = HALF_W ** -0.5 * math.log2(math.e)
VMEM_LIMIT = 56 * 1024 * 1024


def _params(sem):
    return pltpu.CompilerParams(dimension_semantics=sem, vmem_limit_bytes=VMEM_LIMIT)


def _rms(x, g):
    return x * lax.rsqrt(jnp.mean(x * x, axis=-1, keepdims=True) + EPS) * g


def _lam(lq1, lk1, lq2, lk2):
    a = jnp.sum(lq1[...] * lk1[...], axis=-1, keepdims=True)
    b = jnp.sum(lq2[...] * lk2[...], axis=-1, keepdims=True)
    return jnp.exp(a) - jnp.exp(b) + LAM_INIT


MXU_N = 512


def _tril_weights(ws_ref, h):
    r = lax.broadcasted_iota(jnp.int32, (CHUNK, CHUNK), 0)
    c = lax.broadcasted_iota(jnp.int32, (CHUNK, CHUNK), 1)
    return jnp.where(c <= r, ws_ref[h], 0.0).astype(BF16)


N_GROUPS = 5


def _in_proj_kernel(x_ref, g_ref, w_ref, gva_ref, ws_ref, bt_ref, *rest, rows, emit_va, cast_w,
                    resident, q_t):
    rest = list(rest)
    a_ref = rest.pop(0)
    va_ref = rest.pop(0) if emit_va else None
    q_ref, k_ref, v_ref, kb_ref, vb_ref = rest[:5]
    rest = rest[5:]
    wb_ref = rest.pop(0) if cast_w else w_ref
    xg_ref, r_ref, u_ref = rest
    tm = xg_ref.shape[0]
    hpc = MXU_N // HEAD_W

    if cast_w:
        wb_ref[...] = w_ref[...].astype(BF16)

    def project(group, c):
        col = (group * GROUP_W if resident else 0) + c * MXU_N
        z = jnp.dot(xg_ref[...], wb_ref[:, col:col + MXU_N], preferred_element_type=F32)
        return z * r_ref[...]

    def chunks():
        return [(c, slice(c * MXU_N, (c + 1) * MXU_N)) for c in range(GROUP_W // MXU_N)]

    def group_u():
        x = x_ref[...]
        xg_ref[...] = (x * g_ref[...]).astype(BF16)
        r_ref[...] = lax.rsqrt(jnp.mean(x * x, axis=-1, keepdims=True) + EPS)
        for c, cs in chunks():
            u_ref[:, cs] = jax.nn.gelu(project(0, c), approximate=True).astype(u_ref.dtype)

    def group_va():
        for c, _ in chunks():
            a = jax.nn.gelu(project(1, c), approximate=True)
            for hh in range(hpc):
                h = c * hpc + hh
                sl = slice(h * HEAD_W, (h + 1) * HEAD_W)
                va = _rms(a[:, hh * HEAD_W:(hh + 1) * HEAD_W], gva_ref[h:h + 1, :])
                if emit_va:
                    va_ref[:, sl] = va
                w = _tril_weights(ws_ref, h)
                bias = bt_ref[:, h:h + 1]
                for t in range(tm // rows):
                    rs = slice(t * rows, (t + 1) * rows)
                    vc = va[rs]
                    if rows < CHUNK:
                        vc = jnp.concatenate([vc, jnp.zeros((CHUNK - rows, HEAD_W), F32)], axis=0)
                    mixed = jnp.dot(w, vc.astype(BF16), preferred_element_type=F32) + bias
                    a_ref[rs, sl] = (u_ref[rs, sl].astype(F32) * mixed[:rows]).astype(a_ref.dtype)

    def group_q():
        for c, cs in chunks():
            z = project(2, c) * Q_SCALE
            if q_t:
                for hh in range(hpc):
                    q_ref[c * hpc + hh] = z[:, hh * HEAD_W:(hh + 1) * HEAD_W].T.astype(q_ref.dtype)
            else:
                q_ref[:, cs] = z.astype(q_ref.dtype)

    def group_kv(group, f32_ref, bf16_ref):
        for c, cs in chunks():
            z = project(group, c)
            f32_ref[:, cs] = z
            bf16_ref[:, cs] = z.astype(BF16)

    groups = [group_u, group_va, group_q,
              functools.partial(group_kv, 3, k_ref, kb_ref),
              functools.partial(group_kv, 4, v_ref, vb_ref)]
    if resident:
        for run in groups:
            run()
    else:
        for j, run in enumerate(groups):
            pl.when(pl.program_id(1) == j)(run)


def _in_proj(x2d, g, w, gva, w_s, b_t, tm, rows, act_dtype, emit_va, cast_w, resident, q_t):
    m, d = x2d.shape
    row = lambda i, j: (i, 0)
    const = lambda i, j: (0, 0)
    blk = pl.BlockSpec((tm, GROUP_W), row)
    if resident:
        wspec = pl.BlockSpec(w.shape, const, pipeline_mode=pl.Buffered(1))
    else:
        wspec = pl.BlockSpec((d, GROUP_W), lambda i, j: (0, j))
    act = jax.ShapeDtypeStruct((m, GROUP_W), act_dtype)
    out_shape = ([act] + ([jax.ShapeDtypeStruct((m, GROUP_W), F32)] if emit_va else [])
                 + [act, jax.ShapeDtypeStruct((m, GROUP_W), F32),
                    jax.ShapeDtypeStruct((m, GROUP_W), F32),
                    jax.ShapeDtypeStruct((m, GROUP_W), BF16),
                    jax.ShapeDtypeStruct((m, GROUP_W), BF16)])
    out_specs = [blk] * len(out_shape)
    if q_t:
        q_pos = 2 if emit_va else 1
        out_shape[q_pos] = jax.ShapeDtypeStruct((N_HEADS, HEAD_W, m), act_dtype)
        out_specs[q_pos] = pl.BlockSpec((N_HEADS, HEAD_W, tm), lambda i, j: (0, 0, i))
    if cast_w:
        assert m == tm and not resident, "the bf16 weight copy is written once per column group"
        out_shape.append(jax.ShapeDtypeStruct(w.shape, BF16))
        out_specs.append(wspec)
    return pl.pallas_call(
        functools.partial(_in_proj_kernel, rows=rows, emit_va=emit_va, cast_w=cast_w,
                          resident=resident, q_t=q_t),
        grid=(m // tm, 1 if resident else N_GROUPS),
        in_specs=[pl.BlockSpec((tm, d), row), pl.BlockSpec((1, d), const), wspec,
                  pl.BlockSpec((N_HEADS, HEAD_W), const),
                  pl.BlockSpec((N_HEADS, CHUNK, CHUNK), lambda i, j: (0, 0, 0)),
                  pl.BlockSpec((CHUNK, N_HEADS), const)],
        out_specs=out_specs,
        out_shape=out_shape,
        scratch_shapes=[pltpu.VMEM((tm, d), BF16), pltpu.VMEM((tm, 1), F32),
                        pltpu.VMEM((tm, GROUP_W), act_dtype)],
        compiler_params=_params(("arbitrary", "arbitrary")),
        name="in_proj",
    )(x2d, g, w, gva, w_s, b_t)


ONES_PAD = 16


def _subln(o, lam, g):
    t = o.shape[0] // 2
    d = o[:t] - lam * o[t:]
    return _rms(d, g) * (1.0 - LAM_INIT)


def _prompt_attn_kernel(lq1, lk1, lq2, lk2, g_ref, q_ref, k_ref, v_ref, o_ref,
                        qt_ref, vt_ref, m_ref, acc_ref, s_ref, *, tq, n_chain, n_head):
    qi = pl.program_id(2)

    def lanes(hh):
        return slice(hh * HEAD_W, (hh + 1) * HEAD_W)

    @pl.when(qi == 0)
    def _():
        ones_row = lax.broadcasted_iota(jnp.int32, (ONES_PAD, tq), 0) == 0
        for hh in range(n_head):
            for c in range(vt_ref.shape[1]):
                vt_ref[hh, c, :HEAD_W, :] = v_ref[0, c * tq:(c + 1) * tq, lanes(hh)].astype(
                    F32).T.astype(BF16)
                vt_ref[hh, c, HEAD_W:, :] = jnp.where(ones_row, 1.0, 0.0).astype(BF16)

    for hh in range(n_head):
        qf = q_ref[hh]
        dim = lax.broadcasted_iota(jnp.int32, qf.shape, 0)
        zero = jnp.zeros_like(qf)
        qt_ref[hh, :, :tq] = jnp.where(dim < HALF_W, qf, zero)
        qt_ref[hh, :, tq:] = jnp.where(dim >= HALF_W, qf, zero)
    m_ref[...] = jnp.full(m_ref.shape, -jnp.inf, F32)
    acc_ref[...] = jnp.zeros(acc_ref.shape, F32)
    cw = 2 * tq // n_chain

    def n_keys(c, diag):
        return min(tq, (c * cw) % tq + cw) if diag else tq

    def scores(kb, slot, diag):
        start = pl.multiple_of(kb * tq, tq)
        for hh in range(n_head):
            k = k_ref[0, pl.ds(start, tq), lanes(hh)]
            for c in range(n_chain):
                cs = slice(c * cw, (c + 1) * cw)
                nk = n_keys(c, diag)
                s = jnp.dot(k[:nk], qt_ref[hh, :, cs], preferred_element_type=F32)
                if diag:
                    key = lax.broadcasted_iota(jnp.int32, s.shape, 0)
                    qry = lax.broadcasted_iota(jnp.int32, s.shape, 1) + (c * cw) % tq
                    s = jnp.where(key <= qry, s, -jnp.inf)
                s_ref[hh, slot, c, :nk, :] = s

    def absorb(kb, slot, diag=False):
        for hh in range(n_head):
            vt = vt_ref[hh, kb]
            for c in range(n_chain):
                cs = slice(c * cw, (c + 1) * cw)
                nk = n_keys(c, diag)
                s = s_ref[hh, slot, c, :nk, :]
                m_old = m_ref[hh, :, cs]
                m_new = jnp.maximum(m_old, jnp.max(s, axis=0, keepdims=True))
                alpha = jnp.exp2(m_old - m_new)
                p = jnp.exp2(s - m_new)
                acc_ref[hh, :, cs] = alpha * acc_ref[hh, :, cs] + jnp.dot(
                    vt[:, :nk], p.astype(BF16), preferred_element_type=F32)
                m_ref[hh, :, cs] = m_new

    @pl.when(qi == 0)
    def _():
        scores(0, 0, True)

    @pl.when(qi > 0)
    def _():
        scores(0, 0, False)

    n_pair = lax.shift_right_logical(jnp.maximum(qi - 1, 0), 1)

    def pair(t, carry):
        kb = 2 * t
        scores(kb + 1, 1, False)
        absorb(kb, 0)
        scores(kb + 2, 0, False)
        absorb(kb + 1, 1)
        return carry

    lax.fori_loop(0, n_pair, pair, 0)
    left = qi - 2 * n_pair

    @pl.when(left == 0)
    def _():
        absorb(qi, 0, True)

    @pl.when(left == 1)
    def _():
        scores(qi, 1, True)
        absorb(qi - 1, 0)
        absorb(qi, 1, True)

    @pl.when(left == 2)
    def _():
        scores(qi - 1, 1, False)
        absorb(qi - 2, 0)
        scores(qi, 0, True)
        absorb(qi - 1, 1)
        absorb(qi, 0, True)

    lam = _lam(lq1, lk1, lq2, lk2)
    for hh in range(n_head):
        o = acc_ref[hh, :HEAD_W, :] / acc_ref[hh, HEAD_W:HEAD_W + 1, :]
        d = o[:, :tq] - lam * o[:, tq:]
        dn = d * lax.rsqrt(jnp.mean(d * d, axis=0, keepdims=True) + EPS)
        o_ref[0, :, lanes(hh)] = (dn.T * g_ref[...] * (1.0 - LAM_INIT)).astype(o_ref.dtype)


def _prompt_attn(q_t, k, v, lams, g_subln, tq, n_chain, n_head):
    b, s, _ = k.shape
    nq = s // tq
    lam_spec = pl.BlockSpec((1, HALF_W), lambda bi, h, qi: (0, 0))
    qtspec = pl.BlockSpec((n_head, HEAD_W, tq), lambda bi, h, qi: (h, 0, bi * nq + qi))
    qspec = pl.BlockSpec((1, tq, n_head * HEAD_W), lambda bi, h, qi: (bi, qi, h))
    kvspec = pl.BlockSpec((1, s, n_head * HEAD_W), lambda bi, h, qi: (bi, 0, h))
    return pl.pallas_call(
        functools.partial(_prompt_attn_kernel, tq=tq, n_chain=n_chain, n_head=n_head),
        grid=(b, N_HEADS // n_head, s // tq),
        in_specs=[lam_spec] * 4 + [pl.BlockSpec((1, HEAD_W), lambda bi, h, qi: (0, 0)),
                                   qtspec, kvspec, kvspec],
        out_specs=qspec,
        out_shape=jax.ShapeDtypeStruct((b, s, GROUP_W), BF16),
        scratch_shapes=[pltpu.VMEM((n_head, HEAD_W, 2 * tq), BF16),
                        pltpu.VMEM((n_head, nq, HEAD_W + ONES_PAD, tq), BF16),
                        pltpu.VMEM((n_head, 1, 2 * tq), F32),
                        pltpu.VMEM((n_head, HEAD_W + ONES_PAD, 2 * tq), F32),
                        pltpu.VMEM((n_head, 2, n_chain, tq, 2 * tq // n_chain), F32)],
        compiler_params=_params(("arbitrary", "arbitrary", "arbitrary")),
        name="prompt_attn",
    )(*lams, g_subln, q_t, k, v)


def _sample_attn_kernel(pt_ref, lq1, lk1, lq2, lk2, g_ref, q_ref, kn_ref, vn_ref, *rest,
                        pages, t_new):
    k_refs = rest[:pages]
    v_refs = rest[pages:2 * pages]
    o_ref = rest[2 * pages]
    qbd_ref, m_ref, l_ref, acc_ref = rest[2 * pages + 1:]
    c = pl.program_id(1)
    nc = pl.num_programs(1)
    n_rows = qbd_ref.shape[0]

    @pl.when(c == 0)
    def _():
        q = q_ref[...].astype(F32)
        qt = jnp.concatenate([q] * (n_rows // t_new), axis=0)
        r = lax.broadcasted_iota(jnp.int32, qt.shape, 0)
        ln = lax.broadcasted_iota(jnp.int32, qt.shape, 1)
        qbd_ref[...] = jnp.where((r // t_new) == (ln // HALF_W), qt, 0.0).astype(BF16)
        m_ref[...] = jnp.full(m_ref.shape, -jnp.inf, F32)
        l_ref[...] = jnp.zeros(l_ref.shape, F32)
        acc_ref[...] = jnp.zeros(acc_ref.shape, F32)

    qbd = qbd_ref[...]

    def online(s_blocks, v_blocks):
        s = jnp.concatenate(s_blocks, axis=1) if len(s_blocks) > 1 else s_blocks[0]
        m_old = m_ref[...]
        m_new = jnp.maximum(m_old, jnp.max(s, axis=-1, keepdims=True))
        alpha = jnp.exp2(m_old - m_new)
        p = jnp.exp2(s - m_new)
        l_ref[...] = alpha * l_ref[...] + jnp.sum(p, axis=-1, keepdims=True)
        pv = None
        off = 0
        for vb in v_blocks:
            n = vb.shape[0]
            d = jnp.dot(p[:, off:off + n].astype(BF16), vb, preferred_element_type=F32)
            pv = d if pv is None else pv + d
            off += n
        acc_ref[...] = alpha * acc_ref[...] + pv
        m_ref[...] = m_new

    def scores(kb):
        return lax.dot_general(qbd, kb, (((1,), (1,)), ((), ())), preferred_element_type=F32)

    def load_page(ref):
        return jnp.concatenate([ref[pl.ds(h, PAGE, stride=N_HEADS), :] for h in range(N_HEADS)],
                               axis=1).astype(BF16)

    online([scores(load_page(r)) for r in k_refs], [load_page(r) for r in v_refs])

    @pl.when(c == nc - 1)
    def _():
        pad = jnp.zeros((PAGE - t_new, GROUP_W), F32)
        kn = jnp.concatenate([kn_ref[...], pad], axis=0).astype(BF16)
        vn = jnp.concatenate([vn_ref[...], pad], axis=0).astype(BF16)
        s = scores(kn)
        r = lax.broadcasted_iota(jnp.int32, s.shape, 0)
        col = lax.broadcasted_iota(jnp.int32, s.shape, 1)
        s = jnp.where(col <= (r % t_new), s, -jnp.inf)
        online([s], [vn])

        lam = _lam(lq1, lk1, lq2, lk2)
        o = acc_ref[...] / l_ref[...]
        for h in range(N_HEADS):
            blk = o[2 * t_new * h:2 * t_new * (h + 1), h * HEAD_W:(h + 1) * HEAD_W]
            o_ref[:, h * HEAD_W:(h + 1) * HEAD_W] = _subln(blk, lam, g_ref[...])


def _sample_attn(q, k_new, v_new, cache_k, cache_v, page_table, lams, g_subln, t_new, pages):
    n_b, n_pages = page_table.shape
    ck = cache_k.reshape(-1, HEAD_W)
    cv = cache_v.reshape(-1, HEAD_W)
    pt = page_table.reshape(-1)
    n_rows = 2 * N_HEADS * t_new
    const = lambda shape: pl.BlockSpec(shape, lambda b, c, pt_ref: (0, 0))
    rows = pl.BlockSpec((t_new, GROUP_W), lambda b, c, pt_ref: (b, 0))

    def page_spec(j):
        return pl.BlockSpec(
            (PAGE * N_HEADS, HEAD_W),
            lambda b, c, pt_ref: (pt_ref[b * n_pages + c * pages + j], 0))

    grid_spec = pltpu.PrefetchScalarGridSpec(
        num_scalar_prefetch=1,
        grid=(n_b, n_pages // pages),
        in_specs=[const((1, HALF_W))] * 4 + [const((1, HEAD_W)), rows, rows, rows]
                 + [page_spec(j) for j in range(pages)] * 2,
        out_specs=rows,
        scratch_shapes=[pltpu.VMEM((n_rows, GROUP_W), BF16),
                        pltpu.VMEM((n_rows, 1), F32),
                        pltpu.VMEM((n_rows, 1), F32),
                        pltpu.VMEM((n_rows, GROUP_W), F32)])
    return pl.pallas_call(
        functools.partial(_sample_attn_kernel, pages=pages, t_new=t_new),
        grid_spec=grid_spec,
        out_shape=jax.ShapeDtypeStruct((n_b * t_new, GROUP_W), F32),
        compiler_params=_params(("arbitrary", "arbitrary")),
        name="sample_attn",
    )(pt, *lams, g_subln, q, k_new, v_new, *([ck] * pages), *([cv] * pages))


OUT_PROJ_SUB = 128


def _out_proj_kernel(a_ref, b_ref, w_ref, x_ref, gpost_ref, gpre_ref, h_ref, hn_ref, *wb_refs):
    if wb_refs:
        wb_ref, = wb_refs
        wb_ref[...] = w_ref[...].astype(BF16)
    else:
        wb_ref = w_ref
    tm = x_ref.shape[0]
    sub = min(tm, OUT_PROJ_SUB)
    for r in range(tm // sub):
        rs = slice(r * sub, (r + 1) * sub)
        mix = jnp.dot(a_ref[rs, :].astype(BF16), wb_ref[:GROUP_W, :], preferred_element_type=F32)
        mix += jnp.dot(b_ref[rs, :].astype(BF16), wb_ref[GROUP_W:, :],
                       preferred_element_type=F32)
        h = x_ref[rs, :] + _rms(mix, gpost_ref[...])
        h_ref[rs, :] = h
        hn_ref[rs, :] = _rms(h, gpre_ref[...]).astype(hn_ref.dtype)


def _out_proj(a, b, w, x2d, g_post, g_pre, tm, cast_w):
    m, d = x2d.shape
    row = lambda i: (i, 0)
    const = lambda i: (0, 0)
    out_specs = [pl.BlockSpec((tm, d), row), pl.BlockSpec((tm, d), row)]
    out_shape = [jax.ShapeDtypeStruct((m, d), F32), jax.ShapeDtypeStruct((m, d), BF16)]
    if cast_w:
        assert m == tm, "the bf16 weight copy is written once"
        out_specs.append(pl.BlockSpec(w.shape, const))
        out_shape.append(jax.ShapeDtypeStruct(w.shape, BF16))
    return pl.pallas_call(
        _out_proj_kernel,
        grid=(m // tm,),
        in_specs=[pl.BlockSpec((tm, GROUP_W), row), pl.BlockSpec((tm, GROUP_W), row),
                  pl.BlockSpec(w.shape, const), pl.BlockSpec((tm, d), row),
                  pl.BlockSpec((1, d), const), pl.BlockSpec((1, d), const)],
        out_specs=out_specs,
        out_shape=out_shape,
        compiler_params=_params(("arbitrary",)),
        name="out_proj",
    )(a, b, w, x2d, g_post, g_pre)


PRE = 8
FFN_SUB = 128


def _gated(cg, cv):
    return (jax.nn.gelu(cg, approximate=True) * cv).astype(BF16)


def _ffn_finish(acc, h_ref, g_ref, o_ref):
    o_ref[...] = h_ref[...] + _rms(acc, g_ref[...])


def _prompt_ffn_kernel(hn_ref, wg_ref, wv_ref, cwg_ref, cwv_ref, cbg_ref, cbv_ref,
                       wd_ref, h_ref, g_ref, o_ref, lastg_ref, lastv_ref,
                       ug_ref, uv_ref, carryg_ref, carryv_ref, acc_ref, *, tm, tiles_per_seq):
    i = pl.program_id(0)
    f = pl.program_id(1)
    nf = pl.num_programs(1)
    nh = CONV_W - 1

    @pl.when((i == 0) & (f == 0))
    def _():
        carryg_ref[...] = jnp.zeros(carryg_ref.shape, F32)
        carryv_ref[...] = jnp.zeros(carryv_ref.shape, F32)

    @pl.when(f == 0)
    def _():
        acc_ref[...] = jnp.zeros(acc_ref.shape, F32)

    def conv(u_ref, cw_ref, cb_ref):
        out = cb_ref[...]
        for j in range(CONV_W):
            lo = PRE - nh + j
            out = out + u_ref[lo:lo + tm, :] * cw_ref[j:j + 1, :]
        return out

    def step(last):
        hn = hn_ref[...]
        starts = (i % tiles_per_seq) == 0
        for u_ref, w_ref, carry_ref, last_ref in ((ug_ref, wg_ref, carryg_ref, lastg_ref),
                                                  (uv_ref, wv_ref, carryv_ref, lastv_ref)):
            prev = carry_ref[f]
            u_ref[PRE - nh:PRE, :] = jnp.where(starts, jnp.zeros_like(prev), prev)
            u_ref[PRE:, :] = jnp.dot(hn, w_ref[...], preferred_element_type=F32)
            tail = u_ref[PRE + tm - nh:, :]
            carry_ref[f] = tail
            last_ref[0] = tail
        act = _gated(conv(ug_ref, cwg_ref, cbg_ref), conv(uv_ref, cwv_ref, cbv_ref))
        if not last:
            acc_ref[...] += jnp.dot(act, wd_ref[...], preferred_element_type=F32)
        else:
            for r in range(tm // FFN_SUB):
                rs = slice(r * FFN_SUB, (r + 1) * FFN_SUB)
                y = acc_ref[rs, :] + jnp.dot(act[rs, :], wd_ref[...], preferred_element_type=F32)
                o_ref[rs, :] = h_ref[rs, :] + _rms(y, g_ref[...])

    pl.when(f < nf - 1)(functools.partial(step, False))
    pl.when(f == nf - 1)(functools.partial(step, True))


def _prompt_ffn(hn, h, w_gate, w_val, w_conv, b_conv, w_down, g_post, tm, tf, seq):
    m, d = h.shape
    d_ff = w_down.shape[0]
    nf = d_ff // tf
    nt = m // tm
    row = lambda i, f: (i, 0)
    const = lambda i, f: (0, 0)
    gate = lambda i, f: (0, f)
    val = lambda i, f: (0, nf + f)
    last = pl.BlockSpec((1, CONV_W - 1, tf), lambda i, f: (i, 0, f))
    return pl.pallas_call(
        functools.partial(_prompt_ffn_kernel, tm=tm, tiles_per_seq=seq // tm),
        grid=(nt, nf),
        in_specs=[pl.BlockSpec((tm, d), row),
                  pl.BlockSpec((d, tf), gate), pl.BlockSpec((d, tf), gate),
                  pl.BlockSpec((CONV_W, tf), gate), pl.BlockSpec((CONV_W, tf), val),
                  pl.BlockSpec((1, tf), gate), pl.BlockSpec((1, tf), val),
                  pl.BlockSpec((tf, d), lambda i, f: (f, 0)),
                  pl.BlockSpec((tm, d), row),
                  pl.BlockSpec((1, d), const)],
        out_specs=[pl.BlockSpec((tm, d), row), last, last],
        out_shape=[jax.ShapeDtypeStruct((m, d), F32),
                   jax.ShapeDtypeStruct((nt, CONV_W - 1, d_ff), F32),
                   jax.ShapeDtypeStruct((nt, CONV_W - 1, d_ff), F32)],
        scratch_shapes=[pltpu.VMEM((PRE + tm, tf), F32),
                        pltpu.VMEM((PRE + tm, tf), F32),
                        pltpu.VMEM((nf, CONV_W - 1, tf), F32),
                        pltpu.VMEM((nf, CONV_W - 1, tf), F32),
                        pltpu.VMEM((tm, d), F32)],
        compiler_params=_params(("arbitrary", "arbitrary")),
        name="prompt_ffn",
    )(hn, w_gate, w_val, w_conv, w_conv, b_conv, b_conv, w_down, h, g_post)


def _sample_ffn_kernel(hn_ref, histg_ref, histv_ref, wg_ref, wv_ref, cwg_ref, cwv_ref,
                       cbg_ref, cbv_ref, wd_ref, h_ref, g_ref, o_ref, newg_ref, newv_ref,
                       wgb_ref, wvb_ref, wdb_ref, acc_ref, hpg_ref, hpv_ref, *, n_b, t_new):
    f = pl.program_id(0)
    nh = CONV_W - 1

    @pl.when(f == 0)
    def _():
        acc_ref[...] = jnp.zeros(acc_ref.shape, F32)

    wgb_ref[...] = wg_ref[...].astype(BF16)
    wvb_ref[...] = wv_ref[...].astype(BF16)
    wdb_ref[...] = wd_ref[...].astype(BF16)
    hn = hn_ref[...]
    ug = jnp.dot(hn, wgb_ref[...], preferred_element_type=F32)
    uv = jnp.dot(hn, wvb_ref[...], preferred_element_type=F32)

    def conv(u, hist_ref, cw_ref, cb_ref, new_ref, hp_ref):
        outs = []
        for b in range(n_b):
            hp_ref[b, t_new - nh:t_new, :] = hist_ref[b]
            hp_ref[b, t_new:, :] = u[b * t_new:(b + 1) * t_new, :]
            c = cb_ref[...]
            for j in range(CONV_W):
                lo = t_new - nh + j
                c = c + hp_ref[b, lo:lo + t_new, :] * cw_ref[j:j + 1, :]
            outs.append(c)
            new_ref[b] = hp_ref[b, 2 * t_new - nh:, :]
        return jnp.concatenate(outs, axis=0)

    act = _gated(conv(ug, histg_ref, cwg_ref, cbg_ref, newg_ref, hpg_ref),
                 conv(uv, histv_ref, cwv_ref, cbv_ref, newv_ref, hpv_ref))
    acc_ref[...] += jnp.dot(act, wdb_ref[...], preferred_element_type=F32)

    @pl.when(f == pl.num_programs(0) - 1)
    def _():
        _ffn_finish(acc_ref[...], h_ref, g_ref, o_ref)


def _sample_ffn(hn, h, hist, w_up, w_conv, b_conv, w_down, g_post, tf, n_b, t_new):
    m, d = h.shape
    d_ff = w_down.shape[0]
    nf = d_ff // tf
    const = lambda f: (0, 0)
    gate = lambda f: (0, f)
    val = lambda f: (0, nf + f)
    hist_g = pl.BlockSpec((n_b, CONV_W - 1, tf), lambda f: (0, 0, f))
    hist_v = pl.BlockSpec((n_b, CONV_W - 1, tf), lambda f: (0, 0, nf + f))
    return pl.pallas_call(
        functools.partial(_sample_ffn_kernel, n_b=n_b, t_new=t_new),
        grid=(nf,),
        in_specs=[pl.BlockSpec((m, d), const), hist_g, hist_v,
                  pl.BlockSpec((d, tf), gate), pl.BlockSpec((d, tf), val),
                  pl.BlockSpec((CONV_W, tf), gate), pl.BlockSpec((CONV_W, tf), val),
                  pl.BlockSpec((1, tf), gate), pl.BlockSpec((1, tf), val),
                  pl.BlockSpec((tf, d), lambda f: (f, 0)),
                  pl.BlockSpec((m, d), const), pl.BlockSpec((1, d), const)],
        out_specs=[pl.BlockSpec((m, d), const), hist_g, hist_g,
                   pl.BlockSpec((d, tf), gate), pl.BlockSpec((d, tf), gate),
                   pl.BlockSpec((tf, d), lambda f: (f, 0))],
        out_shape=[jax.ShapeDtypeStruct((m, d), F32),
                   jax.ShapeDtypeStruct((n_b, CONV_W - 1, d_ff), F32),
                   jax.ShapeDtypeStruct((n_b, CONV_W - 1, d_ff), F32),
                   jax.ShapeDtypeStruct((d, d_ff), BF16),
                   jax.ShapeDtypeStruct((d, d_ff), BF16),
                   jax.ShapeDtypeStruct((d_ff, d), BF16)],
        scratch_shapes=[pltpu.VMEM((m, d), F32),
                        pltpu.VMEM((n_b, 2 * t_new, tf), F32),
                        pltpu.VMEM((n_b, 2 * t_new, tf), F32)],
        compiler_params=_params(("arbitrary",)),
        name="sample_ffn",
    )(hn, hist, hist, w_up, w_up, w_conv, w_conv, b_conv, b_conv, w_down, h, g_post)


def kernel(x_prompt, x_sample, cache_k, cache_v, state_conv, page_table, g_pre_mix, w_in, g_v_a,
           w_s, b_s, lam_q1, lam_k1, lam_q2, lam_k2, g_subln, w_out, g_post_mix, g_pre_ffn, w_up,
           w_conv, b_conv, w_down, g_post_ffn):
    n_bp, seq, d = x_prompt.shape
    n_bs, t_new, _ = x_sample.shape
    depth = w_in.shape[0]
    assert depth == 1

    b_t = b_s[0].T
    lams = (lam_q1, lam_k1, lam_q2, lam_k2)
    ms = n_bs * t_new
    xp = x_prompt.reshape(n_bp * seq, d)
    xs = x_sample.reshape(ms, d)

    a_s, vas, qs, ks, vs, _, _, w_in_b = _in_proj(
        xs, g_pre_mix, w_in[0], g_v_a[0], w_s[0], b_t, ms, t_new, F32, True, True, False, False)
    a_out, q_t, k, v, kb, vb = _in_proj(
        xp, g_pre_mix, w_in_b, g_v_a[0], w_s[0], b_t, 512, CHUNK, BF16, False, False, True, True)

    b_sm = _sample_attn(qs, ks, vs, cache_k, cache_v, page_table, lams, g_subln, t_new, 16)
    sh3 = (n_bp, seq, GROUP_W)
    b_out = _prompt_attn(q_t, kb.reshape(sh3), vb.reshape(sh3), lams, g_subln, 512, 4, 4)

    hs, hns, w_out_b = _out_proj(a_s, b_sm, w_out[0], xs, g_post_mix, g_pre_ffn, ms, True)
    h, hn = _out_proj(a_out, b_out.reshape(n_bp * seq, GROUP_W), w_out_b, xp,
                      g_post_mix, g_pre_ffn, 512, False)

    ys, newg, newv, w_gate_b, w_val_b, w_down_b = _sample_ffn(
        hns, hs, state_conv[0], w_up[0], w_conv[0], b_conv, w_down[0], g_post_ffn, 512, n_bs, t_new)
    conv_s = jnp.concatenate([newg, newv], axis=-1)
    tm_ffn = 512
    yp, lastg, lastv = _prompt_ffn(hn, h, w_gate_b, w_val_b, w_conv[0], b_conv, w_down_b,
                                   g_post_ffn, tm_ffn, 512, seq)
    tps = seq // tm_ffn
    conv_p = jnp.concatenate([lastg[tps - 1::tps], lastv[tps - 1::tps]], axis=-1)

    hsh = (depth, n_bp, seq, N_HEADS, HEAD_W)
    ssh = (depth, n_bs, t_new, N_HEADS, HEAD_W)
    return (yp.reshape(n_bp, seq, d), ys.reshape(n_bs, t_new, d),
            k.reshape(hsh), v.reshape(hsh), conv_p[None],
            ks.reshape(ssh), vs.reshape(ssh), conv_s[None], vas.reshape(ssh))
```

```python
import functools
import math

import jax
import jax.numpy as jnp
from jax import lax
from jax.experimental import pallas as pl
from jax.experimental.pallas import tpu as pltpu

F32 = jnp.float32
BF16 = jnp.bfloat16

EPS = 1e-6
HEAD_W = 128
HALF_W = HEAD_W // 2
N_HEADS = 8
GROUP_W = N_HEADS * HEAD_W
CHUNK = 128
PAGE = 128
CONV_W = 3
LAM_INIT = 0.8 - 0.6 * math.exp(0.0)
Q_SCALE = HALF_W ** -0.5 * math.log2(math.e)
VMEM_LIMIT = 56 * 1024 * 1024


def _params(sem):
    return pltpu.CompilerParams(dimension_semantics=sem, vmem_limit_bytes=VMEM_LIMIT)


def _rms(x, g):
    return x * lax.rsqrt(jnp.mean(x * x, axis=-1, keepdims=True) + EPS) * g


def _lam(lq1, lk1, lq2, lk2):
    a = jnp.sum(lq1[...] * lk1[...], axis=-1, keepdims=True)
    b = jnp.sum(lq2[...] * lk2[...], axis=-1, keepdims=True)
    return jnp.exp(a) - jnp.exp(b) + LAM_INIT


MXU_N = 512


def _tril_weights(ws_ref, h):
    r = lax.broadcasted_iota(jnp.int32, (CHUNK, CHUNK), 0)
    c = lax.broadcasted_iota(jnp.int32, (CHUNK, CHUNK), 1)
    return jnp.where(c <= r, ws_ref[h], 0.0).astype(BF16)


N_GROUPS = 5
ONES_PAD = 16


def _in_proj_kernel(x_ref, g_ref, w_ref, gva_ref, ws_ref, bt_ref, *rest, rows, emit_va, cast_w,
                    resident, q_t):
    rest = list(rest)
    a_ref = rest.pop(0)
    va_ref = rest.pop(0) if emit_va else None
    q_ref, k_ref, v_ref, kb_ref, vb_ref = rest[:5]
    rest = rest[5:]
    wb_ref = rest.pop(0) if cast_w else w_ref
    xg_ref, r_ref, u_ref = rest
    tm = xg_ref.shape[0]
    hpc = MXU_N // HEAD_W

    if cast_w:
        wb_ref[...] = w_ref[...].astype(BF16)

    def project(group, c):
        col = (group * GROUP_W if resident else 0) + c * MXU_N
        z = jnp.dot(xg_ref[...], wb_ref[:, col:col + MXU_N], preferred_element_type=F32)
        return z * r_ref[...]

    def chunks():
        return [(c, slice(c * MXU_N, (c + 1) * MXU_N)) for c in range(GROUP_W // MXU_N)]

    def group_u():
        x = x_ref[...]
        xg_ref[...] = (x * g_ref[...]).astype(BF16)
        r_ref[...] = lax.rsqrt(jnp.mean(x * x, axis=-1, keepdims=True) + EPS)
        for c, cs in chunks():
            u_ref[:, cs] = jax.nn.gelu(project(0, c), approximate=True).astype(u_ref.dtype)

    def group_va():
        for c, _ in chunks():
            a = jax.nn.gelu(project(1, c), approximate=True)
            for hh in range(hpc):
                h = c * hpc + hh
                sl = slice(h * HEAD_W, (h + 1) * HEAD_W)
                va = _rms(a[:, hh * HEAD_W:(hh + 1) * HEAD_W], gva_ref[h:h + 1, :])
                if emit_va:
                    va_ref[:, sl] = va
                w = _tril_weights(ws_ref, h)
                bias = bt_ref[:, h:h + 1]
                for t in range(tm // rows):
                    rs = slice(t * rows, (t + 1) * rows)
                    vc = va[rs]
                    if rows < CHUNK:
                        vc = jnp.concatenate([vc, jnp.zeros((CHUNK - rows, HEAD_W), F32)], axis=0)
                    mixed = jnp.dot(w, vc.astype(BF16), preferred_element_type=F32) + bias
                    a_ref[rs, sl] = (u_ref[rs, sl].astype(F32) * mixed[:rows]).astype(a_ref.dtype)

    def group_q():
        for c, cs in chunks():
            z = project(2, c) * Q_SCALE
            if q_t:
                for hh in range(hpc):
                    q_ref[c * hpc + hh] = z[:, hh * HEAD_W:(hh + 1) * HEAD_W].T.astype(q_ref.dtype)
            else:
                q_ref[:, cs] = z.astype(q_ref.dtype)

    def group_kv(group, f32_ref, bf16_ref):
        for c, cs in chunks():
            z = project(group, c)
            f32_ref[:, cs] = z
            if q_t and group == 4:
                ones_row = lax.broadcasted_iota(jnp.int32, (ONES_PAD, tm), 0) == 0
                for hh in range(hpc):
                    h = c * hpc + hh
                    bf16_ref[h, 0, :HEAD_W, :] = z[:, hh * HEAD_W:(hh + 1) * HEAD_W].T.astype(BF16)
                    bf16_ref[h, 0, HEAD_W:, :] = jnp.where(ones_row, 1.0, 0.0).astype(BF16)
            else:
                bf16_ref[:, cs] = z.astype(BF16)

    groups = [group_u, group_va, group_q,
              functools.partial(group_kv, 3, k_ref, kb_ref),
              functools.partial(group_kv, 4, v_ref, vb_ref)]
    if resident:
        for run in groups:
            run()
    else:
        for j, run in enumerate(groups):
            pl.when(pl.program_id(1) == j)(run)


def _in_proj(x2d, g, w, gva, w_s, b_t, tm, rows, act_dtype, emit_va, cast_w, resident, q_t):
    m, d = x2d.shape
    row = lambda i, j: (i, 0)
    const = lambda i, j: (0, 0)
    blk = pl.BlockSpec((tm, GROUP_W), row)
    if resident:
        wspec = pl.BlockSpec(w.shape, const, pipeline_mode=pl.Buffered(1))
    else:
        wspec = pl.BlockSpec((d, GROUP_W), lambda i, j: (0, j))
    act = jax.ShapeDtypeStruct((m, GROUP_W), act_dtype)
    out_shape = ([act] + ([jax.ShapeDtypeStruct((m, GROUP_W), F32)] if emit_va else [])
                 + [act, jax.ShapeDtypeStruct((m, GROUP_W), F32),
                    jax.ShapeDtypeStruct((m, GROUP_W), F32),
                    jax.ShapeDtypeStruct((m, GROUP_W), BF16),
                    jax.ShapeDtypeStruct((m, GROUP_W), BF16)])
    out_specs = [blk] * len(out_shape)
    if q_t:
        q_pos = 2 if emit_va else 1
        out_shape[q_pos] = jax.ShapeDtypeStruct((N_HEADS, HEAD_W, m), act_dtype)
        out_specs[q_pos] = pl.BlockSpec((N_HEADS, HEAD_W, tm), lambda i, j: (0, 0, i))
        out_shape[-1] = jax.ShapeDtypeStruct((N_HEADS, m // tm, HEAD_W + ONES_PAD, tm), BF16)
        out_specs[-1] = pl.BlockSpec((N_HEADS, 1, HEAD_W + ONES_PAD, tm),
                                     lambda i, j: (0, i, 0, 0))
    if cast_w:
        assert m == tm and not resident, "the bf16 weight copy is written once per column group"
        out_shape.append(jax.ShapeDtypeStruct(w.shape, BF16))
        out_specs.append(wspec)
    return pl.pallas_call(
        functools.partial(_in_proj_kernel, rows=rows, emit_va=emit_va, cast_w=cast_w,
                          resident=resident, q_t=q_t),
        grid=(m // tm, 1 if resident else N_GROUPS),
        in_specs=[pl.BlockSpec((tm, d), row), pl.BlockSpec((1, d), const), wspec,
                  pl.BlockSpec((N_HEADS, HEAD_W), const),
                  pl.BlockSpec((N_HEADS, CHUNK, CHUNK), lambda i, j: (0, 0, 0)),
                  pl.BlockSpec((CHUNK, N_HEADS), const)],
        out_specs=out_specs,
        out_shape=out_shape,
        scratch_shapes=[pltpu.VMEM((tm, d), BF16), pltpu.VMEM((tm, 1), F32),
                        pltpu.VMEM((tm, GROUP_W), act_dtype)],
        compiler_params=_params(("arbitrary", "arbitrary")),
        name="in_proj",
    )(x2d, g, w, gva, w_s, b_t)


def _subln(o, lam, g):
    t = o.shape[0] // 2
    d = o[:t] - lam * o[t:]
    return _rms(d, g) * (1.0 - LAM_INIT)


def _prompt_attn_kernel(lq1, lk1, lq2, lk2, g_ref, q_ref, k_ref, v_ref, o_ref,
                        qt_ref, m_ref, acc_ref, s_ref, *, tq, n_chain, n_head):
    qi = pl.program_id(2)

    def lanes(hh):
        return slice(hh * HEAD_W, (hh + 1) * HEAD_W)

    for hh in range(n_head):
        qf = q_ref[hh]
        dim = lax.broadcasted_iota(jnp.int32, qf.shape, 0)
        zero = jnp.zeros_like(qf)
        qt_ref[hh, :, :tq] = jnp.where(dim < HALF_W, qf, zero)
        qt_ref[hh, :, tq:] = jnp.where(dim >= HALF_W, qf, zero)
    m_ref[...] = jnp.full(m_ref.shape, -jnp.inf, F32)
    acc_ref[...] = jnp.zeros(acc_ref.shape, F32)
    cw = 2 * tq // n_chain

    def n_keys(c, diag):
        return min(tq, (c * cw) % tq + cw) if diag else tq

    def scores(kb, slot, diag):
        start = pl.multiple_of(kb * tq, tq)
        for hh in range(n_head):
            k = k_ref[0, pl.ds(start, tq), lanes(hh)]
            for c in range(n_chain):
                cs = slice(c * cw, (c + 1) * cw)
                nk = n_keys(c, diag)
                s = jnp.dot(k[:nk], qt_ref[hh, :, cs], preferred_element_type=F32)
                if diag:
                    key = lax.broadcasted_iota(jnp.int32, s.shape, 0)
                    qry = lax.broadcasted_iota(jnp.int32, s.shape, 1) + (c * cw) % tq
                    s = jnp.where(key <= qry, s, -jnp.inf)
                s_ref[hh, slot, c, :nk, :] = s

    def absorb(kb, slot, diag=False):
        for hh in range(n_head):
            vt = v_ref[hh, kb]
            for c in range(n_chain):
                cs = slice(c * cw, (c + 1) * cw)
                nk = n_keys(c, diag)
                s = s_ref[hh, slot, c, :nk, :]
                m_old = m_ref[hh, :, cs]
                m_new = jnp.maximum(m_old, jnp.max(s, axis=0, keepdims=True))
                alpha = jnp.exp2(m_old - m_new)
                p = jnp.exp2(s - m_new)
                acc_ref[hh, :, cs] = alpha * acc_ref[hh, :, cs] + jnp.dot(
                    vt[:, :nk], p.astype(BF16), preferred_element_type=F32)
                m_ref[hh, :, cs] = m_new

    @pl.when(qi == 0)
    def _():
        scores(0, 0, True)

    @pl.when(qi > 0)
    def _():
        scores(0, 0, False)

    n_pair = lax.shift_right_logical(jnp.maximum(qi - 1, 0), 1)

    def pair(t, carry):
        kb = 2 * t
        scores(kb + 1, 1, False)
        absorb(kb, 0)
        scores(kb + 2, 0, False)
        absorb(kb + 1, 1)
        return carry

    lax.fori_loop(0, n_pair, pair, 0)
    left = qi - 2 * n_pair

    @pl.when(left == 0)
    def _():
        absorb(qi, 0, True)

    @pl.when(left == 1)
    def _():
        scores(qi, 1, True)
        absorb(qi - 1, 0)
        absorb(qi, 1, True)

    @pl.when(left == 2)
    def _():
        scores(qi - 1, 1, False)
        absorb(qi - 2, 0)
        scores(qi, 0, True)
        absorb(qi - 1, 1)
        absorb(qi, 0, True)

    lam = _lam(lq1, lk1, lq2, lk2)
    for hh in range(n_head):
        o = acc_ref[hh, :HEAD_W, :] / acc_ref[hh, HEAD_W:HEAD_W + 1, :]
        d = o[:, :tq] - lam * o[:, tq:]
        dn = d * lax.rsqrt(jnp.mean(d * d, axis=0, keepdims=True) + EPS)
        o_ref[0, :, lanes(hh)] = (dn.T * g_ref[...] * (1.0 - LAM_INIT)).astype(o_ref.dtype)


def _prompt_attn(q_t, k, v_t, lams, g_subln, tq, n_chain, n_head):
    b, s, _ = k.shape
    nq = s // tq
    vtspec = pl.BlockSpec((n_head, nq, HEAD_W + ONES_PAD, tq), lambda bi, h, qi: (h, bi, 0, 0))
    lam_spec = pl.BlockSpec((1, HALF_W), lambda bi, h, qi: (0, 0))
    qtspec = pl.BlockSpec((n_head, HEAD_W, tq), lambda bi, h, qi: (h, 0, bi * nq + qi))
    qspec = pl.BlockSpec((1, tq, n_head * HEAD_W), lambda bi, h, qi: (bi, qi, h))
    kvspec = pl.BlockSpec((1, s, n_head * HEAD_W), lambda bi, h, qi: (bi, 0, h))
    return pl.pallas_call(
        functools.partial(_prompt_attn_kernel, tq=tq, n_chain=n_chain, n_head=n_head),
        grid=(b, N_HEADS // n_head, s // tq),
        in_specs=[lam_spec] * 4 + [pl.BlockSpec((1, HEAD_W), lambda bi, h, qi: (0, 0)),
                                   qtspec, kvspec, vtspec],
        out_specs=qspec,
        out_shape=jax.ShapeDtypeStruct((b, s, GROUP_W), BF16),
        scratch_shapes=[pltpu.VMEM((n_head, HEAD_W, 2 * tq), BF16),
                        pltpu.VMEM((n_head, 1, 2 * tq), F32),
                        pltpu.VMEM((n_head, HEAD_W + ONES_PAD, 2 * tq), F32),
                        pltpu.VMEM((n_head, 2, n_chain, tq, 2 * tq // n_chain), F32)],
        compiler_params=_params(("arbitrary", "arbitrary", "arbitrary")),
        name="prompt_attn",
    )(*lams, g_subln, q_t, k, v_t)


def _sample_attn_kernel(pt_ref, lq1, lk1, lq2, lk2, g_ref, q_ref, kn_ref, vn_ref, *rest,
                        pages, t_new):
    k_refs = rest[:pages]
    v_refs = rest[pages:2 * pages]
    o_ref = rest[2 * pages]
    qbd_ref, m_ref, l_ref, acc_ref = rest[2 * pages + 1:]
    c = pl.program_id(1)
    nc = pl.num_programs(1)
    n_rows = qbd_ref.shape[0]

    @pl.when(c == 0)
    def _():
        q = q_ref[...].astype(F32)
        qt = jnp.concatenate([q] * (n_rows // t_new), axis=0)
        r = lax.broadcasted_iota(jnp.int32, qt.shape, 0)
        ln = lax.broadcasted_iota(jnp.int32, qt.shape, 1)
        qbd_ref[...] = jnp.where((r // t_new) == (ln // HALF_W), qt, 0.0).astype(BF16)
        m_ref[...] = jnp.full(m_ref.shape, -jnp.inf, F32)
        l_ref[...] = jnp.zeros(l_ref.shape, F32)
        acc_ref[...] = jnp.zeros(acc_ref.shape, F32)

    qbd = qbd_ref[...]

    def online(s_blocks, v_blocks):
        s = jnp.concatenate(s_blocks, axis=1) if len(s_blocks) > 1 else s_blocks[0]
        m_old = m_ref[...]
        m_new = jnp.maximum(m_old, jnp.max(s, axis=-1, keepdims=True))
        alpha = jnp.exp2(m_old - m_new)
        p = jnp.exp2(s - m_new)
        l_ref[...] = alpha * l_ref[...] + jnp.sum(p, axis=-1, keepdims=True)
        pv = None
        off = 0
        for vb in v_blocks:
            n = vb.shape[0]
            d = jnp.dot(p[:, off:off + n].astype(BF16), vb, preferred_element_type=F32)
            pv = d if pv is None else pv + d
            off += n
        acc_ref[...] = alpha * acc_ref[...] + pv
        m_ref[...] = m_new

    def scores(kb):
        return lax.dot_general(qbd, kb, (((1,), (1,)), ((), ())), preferred_element_type=F32)

    def load_page(ref):
        return jnp.concatenate([ref[pl.ds(h, PAGE, stride=N_HEADS), :] for h in range(N_HEADS)],
                               axis=1).astype(BF16)

    online([scores(load_page(r)) for r in k_refs], [load_page(r) for r in v_refs])

    @pl.when(c == nc - 1)
    def _():
        pad = jnp.zeros((PAGE - t_new, GROUP_W), F32)
        kn = jnp.concatenate([kn_ref[...], pad], axis=0).astype(BF16)
        vn = jnp.concatenate([vn_ref[...], pad], axis=0).astype(BF16)
        s = scores(kn)
        r = lax.broadcasted_iota(jnp.int32, s.shape, 0)
        col = lax.broadcasted_iota(jnp.int32, s.shape, 1)
        s = jnp.where(col <= (r % t_new), s, -jnp.inf)
        online([s], [vn])

        lam = _lam(lq1, lk1, lq2, lk2)
        o = acc_ref[...] / l_ref[...]
        for h in range(N_HEADS):
            blk = o[2 * t_new * h:2 * t_new * (h + 1), h * HEAD_W:(h + 1) * HEAD_W]
            o_ref[:, h * HEAD_W:(h + 1) * HEAD_W] = _subln(blk, lam, g_ref[...])


def _sample_attn(q, k_new, v_new, cache_k, cache_v, page_table, lams, g_subln, t_new, pages):
    n_b, n_pages = page_table.shape
    ck = cache_k.reshape(-1, HEAD_W)
    cv = cache_v.reshape(-1, HEAD_W)
    pt = page_table.reshape(-1)
    n_rows = 2 * N_HEADS * t_new
    const = lambda shape: pl.BlockSpec(shape, lambda b, c, pt_ref: (0, 0))
    rows = pl.BlockSpec((t_new, GROUP_W), lambda b, c, pt_ref: (b, 0))

    def page_spec(j):
        return pl.BlockSpec(
            (PAGE * N_HEADS, HEAD_W),
            lambda b, c, pt_ref: (pt_ref[b * n_pages + c * pages + j], 0))

    grid_spec = pltpu.PrefetchScalarGridSpec(
        num_scalar_prefetch=1,
        grid=(n_b, n_pages // pages),
        in_specs=[const((1, HALF_W))] * 4 + [const((1, HEAD_W)), rows, rows, rows]
                 + [page_spec(j) for j in range(pages)] * 2,
        out_specs=rows,
        scratch_shapes=[pltpu.VMEM((n_rows, GROUP_W), BF16),
                        pltpu.VMEM((n_rows, 1), F32),
                        pltpu.VMEM((n_rows, 1), F32),
                        pltpu.VMEM((n_rows, GROUP_W), F32)])
    return pl.pallas_call(
        functools.partial(_sample_attn_kernel, pages=pages, t_new=t_new),
        grid_spec=grid_spec,
        out_shape=jax.ShapeDtypeStruct((n_b * t_new, GROUP_W), F32),
        compiler_params=_params(("arbitrary", "arbitrary")),
        name="sample_attn",
    )(pt, *lams, g_subln, q, k_new, v_new, *([ck] * pages), *([cv] * pages))


OUT_PROJ_SUB = 128


def _out_proj_kernel(a_ref, b_ref, w_ref, x_ref, gpost_ref, gpre_ref, h_ref, hn_ref, *wb_refs):
    if wb_refs:
        wb_ref, = wb_refs
        wb_ref[...] = w_ref[...].astype(BF16)
    else:
        wb_ref = w_ref
    tm = x_ref.shape[0]
    sub = min(tm, OUT_PROJ_SUB)
    for r in range(tm // sub):
        rs = slice(r * sub, (r + 1) * sub)
        mix = jnp.dot(a_ref[rs, :].astype(BF16), wb_ref[:GROUP_W, :], preferred_element_type=F32)
        mix += jnp.dot(b_ref[rs, :].astype(BF16), wb_ref[GROUP_W:, :],
                       preferred_element_type=F32)
        h = x_ref[rs, :] + _rms(mix, gpost_ref[...])
        h_ref[rs, :] = h
        hn_ref[rs, :] = _rms(h, gpre_ref[...]).astype(hn_ref.dtype)


def _out_proj(a, b, w, x2d, g_post, g_pre, tm, cast_w):
    m, d = x2d.shape
    row = lambda i: (i, 0)
    const = lambda i: (0, 0)
    out_specs = [pl.BlockSpec((tm, d), row), pl.BlockSpec((tm, d), row)]
    out_shape = [jax.ShapeDtypeStruct((m, d), F32), jax.ShapeDtypeStruct((m, d), BF16)]
    if cast_w:
        assert m == tm, "the bf16 weight copy is written once"
        out_specs.append(pl.BlockSpec(w.shape, const))
        out_shape.append(jax.ShapeDtypeStruct(w.shape, BF16))
    return pl.pallas_call(
        _out_proj_kernel,
        grid=(m // tm,),
        in_specs=[pl.BlockSpec((tm, GROUP_W), row), pl.BlockSpec((tm, GROUP_W), row),
                  pl.BlockSpec(w.shape, const), pl.BlockSpec((tm, d), row),
                  pl.BlockSpec((1, d), const), pl.BlockSpec((1, d), const)],
        out_specs=out_specs,
        out_shape=out_shape,
        compiler_params=_params(("arbitrary",)),
        name="out_proj",
    )(a, b, w, x2d, g_post, g_pre)


PRE = 8
FFN_SUB = 128


def _gated(cg, cv):
    return (jax.nn.gelu(cg, approximate=True) * cv).astype(BF16)


def _ffn_finish(acc, h_ref, g_ref, o_ref):
    o_ref[...] = h_ref[...] + _rms(acc, g_ref[...])


def _prompt_ffn_kernel(hn_ref, wg_ref, wv_ref, cwg_ref, cwv_ref, cbg_ref, cbv_ref,
                       wd_ref, h_ref, g_ref, o_ref, lastg_ref, lastv_ref,
                       ug_ref, uv_ref, carryg_ref, carryv_ref, acc_ref, *, tm, tiles_per_seq):
    i = pl.program_id(0)
    f = pl.program_id(1)
    nf = pl.num_programs(1)
    nh = CONV_W - 1

    @pl.when((i == 0) & (f == 0))
    def _():
        carryg_ref[...] = jnp.zeros(carryg_ref.shape, F32)
        carryv_ref[...] = jnp.zeros(carryv_ref.shape, F32)

    @pl.when(f == 0)
    def _():
        acc_ref[...] = jnp.zeros(acc_ref.shape, F32)

    def conv(u_ref, cw_ref, cb_ref):
        out = cb_ref[...]
        for j in range(CONV_W):
            lo = PRE - nh + j
            out = out + u_ref[lo:lo + tm, :] * cw_ref[j:j + 1, :]
        return out

    def step(last):
        hn = hn_ref[...]
        starts = (i % tiles_per_seq) == 0
        for u_ref, w_ref, carry_ref, last_ref in ((ug_ref, wg_ref, carryg_ref, lastg_ref),
                                                  (uv_ref, wv_ref, carryv_ref, lastv_ref)):
            prev = carry_ref[f]
            u_ref[PRE - nh:PRE, :] = jnp.where(starts, jnp.zeros_like(prev), prev)
            u_ref[PRE:, :] = jnp.dot(hn, w_ref[...], preferred_element_type=F32)
            tail = u_ref[PRE + tm - nh:, :]
            carry_ref[f] = tail
            last_ref[0] = tail
        act = _gated(conv(ug_ref, cwg_ref, cbg_ref), conv(uv_ref, cwv_ref, cbv_ref))
        if not last:
            acc_ref[...] += jnp.dot(act, wd_ref[...], preferred_element_type=F32)
        else:
            for r in range(tm // FFN_SUB):
                rs = slice(r * FFN_SUB, (r + 1) * FFN_SUB)
                y = acc_ref[rs, :] + jnp.dot(act[rs, :], wd_ref[...], preferred_element_type=F32)
                o_ref[rs, :] = h_ref[rs, :] + _rms(y, g_ref[...])

    pl.when(f < nf - 1)(functools.partial(step, False))
    pl.when(f == nf - 1)(functools.partial(step, True))


def _prompt_ffn(hn, h, w_gate, w_val, w_conv, b_conv, w_down, g_post, tm, tf, seq):
    m, d = h.shape
    d_ff = w_down.shape[0]
    nf = d_ff // tf
    nt = m // tm
    row = lambda i, f: (i, 0)
    const = lambda i, f: (0, 0)
    gate = lambda i, f: (0, f)
    val = lambda i, f: (0, nf + f)
    last = pl.BlockSpec((1, CONV_W - 1, tf), lambda i, f: (i, 0, f))
    return pl.pallas_call(
        functools.partial(_prompt_ffn_kernel, tm=tm, tiles_per_seq=seq // tm),
        grid=(nt, nf),
        in_specs=[pl.BlockSpec((tm, d), row),
                  pl.BlockSpec((d, tf), gate), pl.BlockSpec((d, tf), gate),
                  pl.BlockSpec((CONV_W, tf), gate), pl.BlockSpec((CONV_W, tf), val),
                  pl.BlockSpec((1, tf), gate), pl.BlockSpec((1, tf), val),
                  pl.BlockSpec((tf, d), lambda i, f: (f, 0)),
                  pl.BlockSpec((tm, d), row),
                  pl.BlockSpec((1, d), const)],
        out_specs=[pl.BlockSpec((tm, d), row), last, last],
        out_shape=[jax.ShapeDtypeStruct((m, d), F32),
                   jax.ShapeDtypeStruct((nt, CONV_W - 1, d_ff), F32),
                   jax.ShapeDtypeStruct((nt, CONV_W - 1, d_ff), F32)],
        scratch_shapes=[pltpu.VMEM((PRE + tm, tf), F32),
                        pltpu.VMEM((PRE + tm, tf), F32),
                        pltpu.VMEM((nf, CONV_W - 1, tf), F32),
                        pltpu.VMEM((nf, CONV_W - 1, tf), F32),
                        pltpu.VMEM((tm, d), F32)],
        compiler_params=_params(("arbitrary", "arbitrary")),
        name="prompt_ffn",
    )(hn, w_gate, w_val, w_conv, w_conv, b_conv, b_conv, w_down, h, g_post)


def _sample_ffn_kernel(hn_ref, histg_ref, histv_ref, wg_ref, wv_ref, cwg_ref, cwv_ref,
                       cbg_ref, cbv_ref, wd_ref, h_ref, g_ref, o_ref, newg_ref, newv_ref,
                       wgb_ref, wvb_ref, wdb_ref, acc_ref, hpg_ref, hpv_ref, *, n_b, t_new):
    f = pl.program_id(0)
    nh = CONV_W - 1

    @pl.when(f == 0)
    def _():
        acc_ref[...] = jnp.zeros(acc_ref.shape, F32)

    wgb_ref[...] = wg_ref[...].astype(BF16)
    wvb_ref[...] = wv_ref[...].astype(BF16)
    wdb_ref[...] = wd_ref[...].astype(BF16)
    hn = hn_ref[...]
    ug = jnp.dot(hn, wgb_ref[...], preferred_element_type=F32)
    uv = jnp.dot(hn, wvb_ref[...], preferred_element_type=F32)

    def conv(u, hist_ref, cw_ref, cb_ref, new_ref, hp_ref):
        outs = []
        for b in range(n_b):
            hp_ref[b, t_new - nh:t_new, :] = hist_ref[b]
            hp_ref[b, t_new:, :] = u[b * t_new:(b + 1) * t_new, :]
            c = cb_ref[...]
            for j in range(CONV_W):
                lo = t_new - nh + j
                c = c + hp_ref[b, lo:lo + t_new, :] * cw_ref[j:j + 1, :]
            outs.append(c)
            new_ref[b] = hp_ref[b, 2 * t_new - nh:, :]
        return jnp.concatenate(outs, axis=0)

    act = _gated(conv(ug, histg_ref, cwg_ref, cbg_ref, newg_ref, hpg_ref),
                 conv(uv, histv_ref, cwv_ref, cbv_ref, newv_ref, hpv_ref))
    acc_ref[...] += jnp.dot(act, wdb_ref[...], preferred_element_type=F32)

    @pl.when(f == pl.num_programs(0) - 1)
    def _():
        _ffn_finish(acc_ref[...], h_ref, g_ref, o_ref)


def _sample_ffn(hn, h, hist, w_up, w_conv, b_conv, w_down, g_post, tf, n_b, t_new):
    m, d = h.shape
    d_ff = w_down.shape[0]
    nf = d_ff // tf
    const = lambda f: (0, 0)
    gate = lambda f: (0, f)
    val = lambda f: (0, nf + f)
    hist_g = pl.BlockSpec((n_b, CONV_W - 1, tf), lambda f: (0, 0, f))
    hist_v = pl.BlockSpec((n_b, CONV_W - 1, tf), lambda f: (0, 0, nf + f))
    return pl.pallas_call(
        functools.partial(_sample_ffn_kernel, n_b=n_b, t_new=t_new),
        grid=(nf,),
        in_specs=[pl.BlockSpec((m, d), const), hist_g, hist_v,
                  pl.BlockSpec((d, tf), gate), pl.BlockSpec((d, tf), val),
                  pl.BlockSpec((CONV_W, tf), gate), pl.BlockSpec((CONV_W, tf), val),
                  pl.BlockSpec((1, tf), gate), pl.BlockSpec((1, tf), val),
                  pl.BlockSpec((tf, d), lambda f: (f, 0)),
                  pl.BlockSpec((m, d), const), pl.BlockSpec((1, d), const)],
        out_specs=[pl.BlockSpec((m, d), const), hist_g, hist_g,
                   pl.BlockSpec((d, tf), gate), pl.BlockSpec((d, tf), gate),
                   pl.BlockSpec((tf, d), lambda f: (f, 0))],
        out_shape=[jax.ShapeDtypeStruct((m, d), F32),
                   jax.ShapeDtypeStruct((n_b, CONV_W - 1, d_ff), F32),
                   jax.ShapeDtypeStruct((n_b, CONV_W - 1, d_ff), F32),
                   jax.ShapeDtypeStruct((d, d_ff), BF16),
                   jax.ShapeDtypeStruct((d, d_ff), BF16),
                   jax.ShapeDtypeStruct((d_ff, d), BF16)],
        scratch_shapes=[pltpu.VMEM((m, d), F32),
                        pltpu.VMEM((n_b, 2 * t_new, tf), F32),
                        pltpu.VMEM((n_b, 2 * t_new, tf), F32)],
        compiler_params=_params(("arbitrary",)),
        name="sample_ffn",
    )(hn, hist, hist, w_up, w_up, w_conv, w_conv, b_conv, b_conv, w_down, h, g_post)


def kernel(x_prompt, x_sample, cache_k, cache_v, state_conv, page_table, g_pre_mix, w_in, g_v_a,
           w_s, b_s, lam_q1, lam_k1, lam_q2, lam_k2, g_subln, w_out, g_post_mix, g_pre_ffn, w_up,
           w_conv, b_conv, w_down, g_post_ffn):
    n_bp, seq, d = x_prompt.shape
    n_bs, t_new, _ = x_sample.shape
    depth = w_in.shape[0]
    assert depth == 1

    b_t = b_s[0].T
    lams = (lam_q1, lam_k1, lam_q2, lam_k2)
    ms = n_bs * t_new
    xp = x_prompt.reshape(n_bp * seq, d)
    xs = x_sample.reshape(ms, d)

    a_s, vas, qs, ks, vs, _, _, w_in_b = _in_proj(
        xs, g_pre_mix, w_in[0], g_v_a[0], w_s[0], b_t, ms, t_new, F32, True, True, False, False)
    a_out, q_t, k, v, kb, v_t = _in_proj(
        xp, g_pre_mix, w_in_b, g_v_a[0], w_s[0], b_t, 512, CHUNK, BF16, False, False, True, True)

    b_sm = _sample_attn(qs, ks, vs, cache_k, cache_v, page_table, lams, g_subln, t_new, 16)
    sh3 = (n_bp, seq, GROUP_W)
    b_out = _prompt_attn(q_t, kb.reshape(sh3), v_t, lams, g_subln, 512, 4, 4)

    hs, hns, w_out_b = _out_proj(a_s, b_sm, w_out[0], xs, g_post_mix, g_pre_ffn, ms, True)
    h, hn = _out_proj(a_out, b_out.reshape(n_bp * seq, GROUP_W), w_out_b, xp,
                      g_post_mix, g_pre_ffn, 512, False)

    ys, newg, newv, w_gate_b, w_val_b, w_down_b = _sample_ffn(
        hns, hs, state_conv[0], w_up[0], w_conv[0], b_conv, w_down[0], g_post_ffn, 512, n_bs, t_new)
    conv_s = jnp.concatenate([newg, newv], axis=-1)
    tm_ffn = 512
    yp, lastg, lastv = _prompt_ffn(hn, h, w_gate_b, w_val_b, w_conv[0], b_conv, w_down_b,
                                   g_post_ffn, tm_ffn, 512, seq)
    tps = seq // tm_ffn
    conv_p = jnp.concatenate([lastg[tps - 1::tps], lastv[tps - 1::tps]], axis=-1)

    hsh = (depth, n_bp, seq, N_HEADS, HEAD_W)
    ssh = (depth, n_bs, t_new, N_HEADS, HEAD_W)
    return (yp.reshape(n_bp, seq, d), ys.reshape(n_bs, t_new, d),
            k.reshape(hsh), v.reshape(hsh), conv_p[None],
            ks.reshape(ssh), vs.reshape(ssh), conv_s[None], vas.reshape(ssh))
```
